```python
import math
import jax, jax.numpy as jnp
from jax import lax
import numpy as np


D_MODEL = 1024
BATCH = 32
SEQ = 2048
DEPTH = 4
DEC_BATCH = 4
DEC_SEQ = 4096
PAST_LEN = 128

HEAD_DIM = 64
N_HEADS = D_MODEL // HEAD_DIM
N_HEADS_NA = N_HEADS // 4
N_HEADS_DIL = N_HEADS - N_HEADS_NA
D_NA = N_HEADS_NA * HEAD_DIM
D_DIL = N_HEADS_DIL * HEAD_DIM
GRID_W = 64
NA_ROWS = 8
NA_COLS = 16
NA_Q_COLS = 16
NA_K_COLS = NA_Q_COLS + NA_COLS
DIL_PATTERNS = ((128, 1), (512, 4), (2048, 16))
DIL_BLOCK = 64
N_EXPERTS = 16
EXPERT_FF = D_MODEL
CAPACITY_FACTOR = 2
RMS_EPS = 1e-6

kernel_name = 'hybrid_na_dilated_ec_encoder'


def rmsnorm(x, gain):
    xf = x.astype(jnp.float32)
    y = xf * lax.rsqrt(jnp.mean(xf * xf, axis=-1, keepdims=True) + RMS_EPS)
    return (y * gain.astype(jnp.float32)).astype(x.dtype)


def alibi_slopes(n):
    def pow2(m):
        start = 2.0 ** (-8.0 / m)
        return [start ** (i + 1) for i in range(m)]
    if math.log2(n).is_integer():
        s = pow2(n)
    else:
        c = 2 ** int(math.floor(math.log2(n)))
        s = pow2(c) + pow2(2 * c)[0::2][: n - c]
    return np.asarray(s, dtype=np.float32)


def neighborhood_attention(q, k, v, rpb):
    b, s, h, dh = q.shape
    rows = s // GRID_W
    kh = min(NA_ROWS, rows)
    ncb = GRID_W // NA_Q_COLS
    r = np.arange(rows)
    key_rows = np.clip(r - NA_ROWS // 2, 0, rows - kh)[:, None] + np.arange(kh)[None, :]
    row_off = key_rows - r[:, None]
    q_cols = np.arange(ncb)[:, None] * NA_Q_COLS + np.arange(NA_Q_COLS)[None, :]
    key_col0 = np.clip(np.arange(ncb) * NA_Q_COLS - NA_COLS // 2, 0, GRID_W - NA_K_COLS)
    key_cols = key_col0[:, None] + np.arange(NA_K_COLS)[None, :]
    win0 = np.clip(q_cols - NA_COLS // 2, 0, GRID_W - NA_COLS)
    in_win = (key_cols[:, None, :] >= win0[:, :, None]) & (key_cols[:, None, :] < win0[:, :, None] + NA_COLS)
    col_off = key_cols[:, None, :] - q_cols[:, :, None]
    ri = (row_off + NA_ROWS - 1)[:, None, None, :, None]
    ci = np.clip(col_off + NA_COLS - 1, 0, 2 * NA_COLS - 2)[None, :, :, None, :]
    bias = rpb.astype(jnp.float32)[:, ri, ci]
    qg = q.reshape(b, rows, ncb, NA_Q_COLS, h, dh)

    def gather(t):
        t = jnp.take(t.reshape(b, rows, GRID_W, h, dh), key_cols, axis=2)
        return jnp.take(t, key_rows, axis=1)

    kb, vb = gather(k), gather(v)
    sc = jnp.einsum('brcqhd,brkcjhd->bhrcqkj', qg, kb, preferred_element_type=jnp.float32) * (dh ** -0.5)
    sc = jnp.where(in_win[None, :, :, None, :], sc + bias, -jnp.inf)
    p = jax.nn.softmax(sc.reshape(b, h, rows, ncb, NA_Q_COLS, kh * NA_K_COLS), axis=-1)
    p = p.reshape(sc.shape).astype(v.dtype)
    o = jnp.einsum('bhrcqkj,brkcjhd->brcqhd', p, vb)
    return o.reshape(b, s, h, dh)


def dilated_branch(q, k, v, slopes, window, dilation):
    b, s, h, dh = q.shape
    L = s // dilation
    half = window // (2 * dilation)
    n = b * dilation
    nb = -(-L // DIL_BLOCK)
    lp = nb * DIL_BLOCK
    kw = DIL_BLOCK + 2 * half

    def to_res(t):
        return jnp.swapaxes(t.reshape(b, L, dilation, h, dh), 1, 2).reshape(n, L, h, dh)

    qr, kr, vr = to_res(q), to_res(k), to_res(v)
    qb = jnp.pad(qr, ((0, 0), (0, lp - L), (0, 0), (0, 0))).reshape(n, nb, DIL_BLOCK, h, dh)
    key_idx = np.arange(nb)[:, None] * DIL_BLOCK + np.arange(kw)[None, :]
    kpad = ((0, 0), (half, lp - L + half), (0, 0), (0, 0))
    kb = jnp.take(jnp.pad(kr, kpad), key_idx, axis=1)
    vb = jnp.take(jnp.pad(vr, kpad), key_idx, axis=1)
    qpos = np.arange(nb)[:, None] * DIL_BLOCK + np.arange(DIL_BLOCK)[None, :]
    kpos = key_idx - half
    dist = np.abs(qpos[:, :, None] - kpos[:, None, :])
    valid = (dist <= half) & (kpos[:, None, :] >= 0) & (kpos[:, None, :] < L)
    bias = -slopes[None, :, None, None] * (dist * dilation).astype(np.float32)[:, None]
    sc = jnp.einsum('nbqhd,nbkhd->nbhqk', qb, kb, preferred_element_type=jnp.float32) * (dh ** -0.5)
    sc = jnp.where(valid[:, None], sc + bias, -jnp.inf)
    m = jnp.max(sc, axis=-1, keepdims=True)
    e = jnp.exp(sc - m)
    den = jnp.sum(e, axis=-1, keepdims=True)
    o = jnp.einsum('nbhqk,nbkhd->nbqhd', (e / den).astype(v.dtype), vb).reshape(n, lp, h, dh)[:, :L]
    lse = jnp.swapaxes((m + jnp.log(den))[..., 0], 2, 3).reshape(n, lp, h)[:, :L]

    def from_res(t):
        rest = t.shape[2:]
        return jnp.swapaxes(t.reshape((b, dilation, L) + rest), 1, 2).reshape((b, s) + rest)

    return from_res(o), from_res(lse)


def dilated_attention(q, k, v, slopes):
    outs, lses = [], []
    for window, dilation in DIL_PATTERNS:
        o, l = dilated_branch(q, k, v, slopes, window, dilation)
        outs.append(o)
        lses.append(l)
    w = jax.nn.softmax(jnp.stack(lses), axis=0)
    o = jnp.sum(w[..., None] * jnp.stack(outs).astype(jnp.float32), axis=0)
    return o.astype(q.dtype)


def token_mixer(h, w_in, rpb, gain_na, gain_dil, w_out, slopes):
    b, s, _ = h.shape
    qkv = jnp.einsum('bsd,de->bse', h, w_in).reshape(b, s, 3, N_HEADS, HEAD_DIM)
    q, k, v = qkv[:, :, 0], qkv[:, :, 1], qkv[:, :, 2]
    a = neighborhood_attention(q[:, :, :N_HEADS_NA], k[:, :, :N_HEADS_NA], v[:, :, :N_HEADS_NA], rpb)
    d = dilated_attention(q[:, :, N_HEADS_NA:], k[:, :, N_HEADS_NA:], v[:, :, N_HEADS_NA:], slopes)
    a = rmsnorm(a, gain_na.reshape(N_HEADS_NA, HEAD_DIM))
    d = rmsnorm(d, gain_dil.reshape(N_HEADS_DIL, HEAD_DIM))
    o = jnp.concatenate([a, d], axis=2).reshape(b, s, D_MODEL)
    return jnp.einsum('bsd,de->bse', o, w_out)


def expert_choice_ffn(h, w_router, w_gate, w_up, w_down):
    b, s, d = h.shape
    n = b * s
    cap = CAPACITY_FACTOR * n // N_EXPERTS
    x = h.reshape(n, d)
    aff = jax.nn.softmax(jnp.einsum('nd,de->ne', x, w_router).astype(jnp.float32), axis=-1)
    g, idx = lax.top_k(aff.T, cap)
    xe = x[idx]
    hid = jax.nn.silu(jnp.einsum('ecd,edf->ecf', xe, w_gate)) * jnp.einsum('ecd,edf->ecf', xe, w_up)
    ye = jnp.einsum('ecf,efd->ecd', hid, w_down) * g[..., None].astype(x.dtype)
    y = jnp.zeros_like(x).at[idx.reshape(-1)].add(ye.reshape(-1, d))
    return y.reshape(b, s, d)


def trunk(x, norm1, w_in, rpb, gain_na, gain_dil, w_out, norm2, w_router, w_gate, w_up, w_down, final_norm):
    slopes = alibi_slopes(N_HEADS_DIL)
    for l in range(DEPTH):
        x = x + token_mixer(rmsnorm(x, norm1[l]), w_in[l], rpb[l], gain_na[l], gain_dil[l], w_out[l], slopes)
        x = x + expert_choice_ffn(rmsnorm(x, norm2[l]), w_router[l], w_gate[l], w_up[l], w_down[l])
    return rmsnorm(x, final_norm)


def setup_inputs(seed: int = 0) -> dict:
    key = jax.random.key(seed)
    ks = jax.random.split(key, 14)
    f32 = jnp.float32
    nrm = jax.random.normal
    return {
        'x_prompt': nrm(ks[0], (BATCH, SEQ, D_MODEL), f32),
        'x_sample': nrm(ks[1], (DEC_BATCH, DEC_SEQ, D_MODEL), f32),
        'norm1': 1.0 + 0.01 * nrm(ks[2], (DEPTH, D_MODEL), f32),
        'w_in': nrm(ks[3], (DEPTH, D_MODEL, 3 * D_MODEL), f32) * D_MODEL ** -0.5,
        'rpb': 0.1 * nrm(ks[4], (DEPTH, N_HEADS_NA, 2 * NA_ROWS - 1, 2 * NA_COLS - 1), f32),
        'gain_na': 1.0 + 0.01 * nrm(ks[5], (DEPTH, D_NA), f32),
        'gain_dil': 1.0 + 0.01 * nrm(ks[6], (DEPTH, D_DIL), f32),
        'w_out': nrm(ks[7], (DEPTH, D_MODEL, D_MODEL), f32) * D_MODEL ** -0.5,
        'norm2': 1.0 + 0.01 * nrm(ks[8], (DEPTH, D_MODEL), f32),
        'w_router': nrm(ks[9], (DEPTH, D_MODEL, N_EXPERTS), f32) * D_MODEL ** -0.5,
        'w_gate': nrm(ks[10], (DEPTH, N_EXPERTS, D_MODEL, EXPERT_FF), f32) * D_MODEL ** -0.5,
        'w_up': nrm(ks[11], (DEPTH, N_EXPERTS, D_MODEL, EXPERT_FF), f32) * D_MODEL ** -0.5,
        'w_down': nrm(ks[12], (DEPTH, N_EXPERTS, EXPERT_FF, D_MODEL), f32) * EXPERT_FF ** -0.5,
        'final_norm': 1.0 + 0.01 * nrm(ks[13], (D_MODEL,), f32),
    }


def reference(x_prompt, x_sample, norm1, w_in, rpb, gain_na, gain_dil, w_out, norm2, w_router, w_gate, w_up, w_down, final_norm):
    y_prompt = trunk(x_prompt, norm1, w_in, rpb, gain_na, gain_dil, w_out, norm2, w_router, w_gate, w_up, w_down, final_norm)
    y_sample = trunk(x_sample, norm1, w_in, rpb, gain_na, gain_dil, w_out, norm2, w_router, w_gate, w_up, w_down, final_norm)
    return (y_prompt, y_sample)
```

```python
import functools
import math

import jax
import jax.numpy as jnp
import numpy as np
from jax import lax
from jax.experimental import pallas as pl
from jax.experimental.pallas import tpu as pltpu

D_MODEL = 1024
DEPTH = 4
HEAD_DIM = 64
N_HEADS = D_MODEL // HEAD_DIM
N_HEADS_NA = N_HEADS // 4
N_HEADS_DIL = N_HEADS - N_HEADS_NA
D_NA = N_HEADS_NA * HEAD_DIM
D_DIL = N_HEADS_DIL * HEAD_DIM
GRID_W = 64
NA_ROWS = 8
NA_COLS = 16
DIL_PATTERNS = ((128, 1), (512, 4), (2048, 16))
N_EXPERTS = 16
CAPACITY_FACTOR = 2
RMS_EPS = 1e-6

LANES = 128
HEADS_PER_LANE_TILE = LANES // HEAD_DIM
DIL_GROUP = 256
N_DIL_GROUPS = D_DIL // DIL_GROUP
NEG_BIG = -1e30
VMEM_LIMIT = 56 * 1024 * 1024

BF16 = jnp.bfloat16
F32 = jnp.float32


def _alibi_slopes(n):
    def pow2(m):
        start = 2.0 ** (-8.0 / m)
        return [start ** (i + 1) for i in range(m)]
    if math.log2(n).is_integer():
        s = pow2(n)
    else:
        c = 2 ** int(math.floor(math.log2(n)))
        s = pow2(c) + pow2(2 * c)[0::2][: n - c]
    return np.asarray(s, dtype=np.float32)


def _params(*sem):
    return pltpu.CompilerParams(dimension_semantics=sem, vmem_limit_bytes=VMEM_LIMIT)


def _rms_scale(xf):
    return lax.rsqrt(jnp.mean(xf * xf, axis=-1, keepdims=True) + RMS_EPS)


def _qkv_kernel(x_ref, g_ref, w_ref, o_ref):
    xf = x_ref[...]
    h = (xf * _rms_scale(xf) * g_ref[...]).astype(BF16)
    for c in range(3):
        y = jnp.dot(h, w_ref[:, c * D_MODEL:(c + 1) * D_MODEL], preferred_element_type=F32)
        if c == 0:
            y = y * (HEAD_DIM ** -0.5)
        o_ref[:, c * D_MODEL:(c + 1) * D_MODEL] = y.astype(BF16)


def _qkv_proj(x2, gain, w_bf16, tm=512):
    n = x2.shape[0]
    return pl.pallas_call(
        _qkv_kernel,
        out_shape=jax.ShapeDtypeStruct((n, 3 * D_MODEL), BF16),
        grid=(n // tm,),
        in_specs=[
            pl.BlockSpec((tm, D_MODEL), lambda i: (i, 0)),
            pl.BlockSpec((1, D_MODEL), lambda i: (0, 0)),
            pl.BlockSpec((D_MODEL, 3 * D_MODEL), lambda i: (0, 0)),
        ],
        out_specs=pl.BlockSpec((tm, 3 * D_MODEL), lambda i: (i, 0)),
        compiler_params=_params("parallel"),
        name="qkv_proj",
    )(x2, gain.reshape(1, D_MODEL), w_bf16)


def _na_bias_table(rpb):
    qc = np.arange(GRID_W)[:, None]
    kc = np.arange(GRID_W)[None, :]
    win0 = np.clip(qc - NA_COLS // 2, 0, GRID_W - NA_COLS)
    in_win = (kc >= win0) & (kc < win0 + NA_COLS)
    ci = np.clip(kc - qc + NA_COLS - 1, 0, 2 * NA_COLS - 2)
    ri = np.arange(NA_ROWS)[:, None] + np.arange(NA_ROWS)[None, :]
    t = rpb.astype(F32)[:, ri[:, :, None, None], ci[None, None]]
    t = jnp.where(in_win[None, None, None], t, NEG_BIG)
    t = jnp.transpose(t, (0, 1, 3, 2, 4))
    return t.reshape(N_HEADS_NA, NA_ROWS, GRID_W, NA_ROWS * GRID_W)


def _head_rms(o, gain_tile, lo):
    o2 = o * o
    ms_lo = jnp.sum(jnp.where(lo, o2, 0.0), axis=-1, keepdims=True) * (1.0 / HEAD_DIM)
    ms_hi = jnp.sum(jnp.where(lo, 0.0, o2), axis=-1, keepdims=True) * (1.0 / HEAD_DIM)
    scale = jnp.where(lo, lax.rsqrt(ms_lo + RMS_EPS), lax.rsqrt(ms_hi + RMS_EPS))
    return o * scale * gain_tile


def _na_kernel(q_ref, k_ref, v_ref, t_ref, g_ref, o_ref, *, rows):
    nk = NA_ROWS * GRID_W
    lo = lax.broadcasted_iota(jnp.int32, (1, LANES), 1) < HEAD_DIM

    def row_body(r, carry):
        kr0 = jnp.clip(r - NA_ROWS // 2, 0, rows - NA_ROWS)
        var = kr0 - r + NA_ROWS - 1
        q0 = pl.multiple_of(r * GRID_W, GRID_W)
        k0 = pl.multiple_of(kr0 * GRID_W, GRID_W)
        for p in range(D_NA // LANES):
            lanes = slice(p * LANES, (p + 1) * LANES)
            q2 = q_ref[0, pl.ds(q0, GRID_W), lanes]
            k2 = k_ref[0, pl.ds(k0, nk), lanes]
            v2 = v_ref[0, pl.ds(k0, nk), lanes]
            outs = []
            for hh in range(HEADS_PER_LANE_TILE):
                h = p * HEADS_PER_LANE_TILE + hh
                qm = jnp.where(lo if hh == 0 else ~lo, q2, jnp.zeros_like(q2))
                s = lax.dot_general(qm, k2, (((1,), (1,)), ((), ())), preferred_element_type=F32)
                s = s + t_ref[h, var]
                m = jnp.max(s, axis=-1, keepdims=True)
                e = jnp.exp(s - m)
                l = jnp.sum(e, axis=-1, keepdims=True)
                o = jnp.dot(e.astype(BF16), v2, preferred_element_type=F32)
                outs.append(o / l)
            o = jnp.where(lo, outs[0], outs[1])
            o_ref[0, pl.ds(q0, GRID_W), lanes] = _head_rms(o, g_ref[:, lanes], lo).astype(BF16)
        return carry

    lax.fori_loop(0, rows, row_body, 0)


def _na_attention(qkv, table, gain_na):
    b, s, _ = qkv.shape
    rows = s // GRID_W
    nblk = D_MODEL // D_NA
    return pl.pallas_call(
        functools.partial(_na_kernel, rows=rows),
        out_shape=jax.ShapeDtypeStruct((b, s, D_NA), BF16),
        grid=(b,),
        in_specs=[
            pl.BlockSpec((1, s, D_NA), lambda i: (i, 0, 0)),
            pl.BlockSpec((1, s, D_NA), lambda i: (i, 0, nblk)),
            pl.BlockSpec((1, s, D_NA), lambda i: (i, 0, 2 * nblk)),
            pl.BlockSpec(table.shape, lambda i: (0, 0, 0, 0)),
            pl.BlockSpec((1, D_NA), lambda i: (0, 0)),
        ],
        out_specs=pl.BlockSpec((1, s, D_NA), lambda i: (i, 0, 0)),
        compiler_params=_params("parallel"),
        name="na_attention",
    )(qkv, qkv, qkv, table, gain_na.reshape(1, D_NA))


def _dil_kernel(q_ref, k_ref, v_ref, o_ref, lse_ref, *, length, dilation, slopes, tq):
    half = 64
    kw = min(length, tq + 2 * half)
    lo = lax.broadcasted_iota(jnp.int32, (1, LANES), 1) < HEAD_DIM
    lane = lax.broadcasted_iota(jnp.int32, (1, LANES), 1)
    grp = pl.program_id(2)
    qi = lax.broadcasted_iota(jnp.int32, (tq, kw), 0)
    ki = lax.broadcasted_iota(jnp.int32, (tq, kw), 1)

    def blk_body(i, carry):
        q0 = pl.multiple_of(i * tq, tq)
        k0 = pl.multiple_of(jnp.clip(q0 - half, 0, length - kw), half)
        dist = jnp.abs(qi + (q0 - k0) - ki)
        valid = dist <= half
        distf = dist.astype(F32) * float(dilation)
        lse_tile = jnp.zeros((tq, LANES), F32)
        for p in range(DIL_GROUP // LANES):
            lanes = slice(p * LANES, (p + 1) * LANES)
            q2 = q_ref[0, pl.ds(q0, tq), lanes]
            k2 = k_ref[0, pl.ds(k0, kw), lanes]
            v2 = v_ref[0, pl.ds(k0, kw), lanes]
            outs = []
            for hh in range(HEADS_PER_LANE_TILE):
                hl = p * HEADS_PER_LANE_TILE + hh
                slope = jnp.float32(slopes[hl])
                for g in range(1, N_DIL_GROUPS):
                    slope = jnp.where(grp == g, jnp.float32(slopes[g * 4 + hl]), slope)
                qm = jnp.where(lo if hh == 0 else ~lo, q2, jnp.zeros_like(q2))
                s = lax.dot_general(qm, k2, (((1,), (1,)), ((), ())), preferred_element_type=F32)
                s = jnp.where(valid, s - slope * distf, NEG_BIG)
                m = jnp.max(s, axis=-1, keepdims=True)
                e = jnp.exp(s - m)
                l = jnp.sum(e, axis=-1, keepdims=True)
                o = jnp.dot(e.astype(BF16), v2, preferred_element_type=F32)
                outs.append(o / l)
                lse_tile = jnp.where(lane == hl, m + jnp.log(l), lse_tile)
            o_ref[0, pl.ds(q0, tq), lanes] = jnp.where(lo, outs[0], outs[1])
        lse_ref[0, pl.ds(q0, tq), :] = lse_tile
        return carry

    lax.fori_loop(0, length // tq, blk_body, 0)


def _dil_attention(qkv, dilation, slopes, tq=128):
    b, s, _ = qkv.shape
    length = s // dilation
    cpt = 3 * D_MODEL // DIL_GROUP
    qoff = D_NA // DIL_GROUP
    per = D_MODEL // DIL_GROUP
    view = qkv.reshape(b, length, dilation * 3 * D_MODEL)

    def spec(part):
        return pl.BlockSpec((1, length, DIL_GROUP),
                            lambda i, r, g: (i, 0, r * cpt + part * per + qoff + g))

    o, lse = pl.pallas_call(
        functools.partial(_dil_kernel, length=length, dilation=dilation,
                          slopes=tuple(float(x) for x in slopes), tq=tq),
        out_shape=(jax.ShapeDtypeStruct((b, length, dilation * D_DIL), F32),
                   jax.ShapeDtypeStruct((b, length, dilation * N_DIL_GROUPS * LANES), F32)),
        grid=(b, dilation, N_DIL_GROUPS),
        in_specs=[spec(0), spec(1), spec(2)],
        out_specs=(pl.BlockSpec((1, length, DIL_GROUP), lambda i, r, g: (i, 0, r * N_DIL_GROUPS + g)),
                   pl.BlockSpec((1, length, LANES), lambda i, r, g: (i, 0, r * N_DIL_GROUPS + g))),
        compiler_params=_params("parallel", "parallel", "parallel"),
        name=f"dil_attention_d{dilation}",
    )(view, view, view)
    return o.reshape(b, s, D_DIL), lse.reshape(b, s, N_DIL_GROUPS * LANES)


def _out_kernel(a_ref, o1_ref, o2_ref, o3_ref, l1_ref, l2_ref, l3_ref, g_ref, w_ref, x_ref, y_ref):
    lo = lax.broadcasted_iota(jnp.int32, (1, LANES), 1) < HEAD_DIM
    acc = x_ref[...] + jnp.dot(a_ref[...], w_ref[0:D_NA, :], preferred_element_type=F32)
    for g in range(N_DIL_GROUPS):
        ls = [r[:, g * LANES:(g + 1) * LANES] for r in (l1_ref, l2_ref, l3_ref)]
        m = jnp.maximum(jnp.maximum(ls[0], ls[1]), ls[2])
        es = [jnp.exp(l - m) for l in ls]
        inv = 1.0 / (es[0] + es[1] + es[2])
        ws = [e * inv for e in es]
        for p in range(DIL_GROUP // LANES):
            c0 = g * DIL_GROUP + p * LANES
            d = jnp.zeros((a_ref.shape[0], LANES), F32)
            for w, o_ref in zip(ws, (o1_ref, o2_ref, o3_ref)):
                wt = jnp.where(lo, w[:, 2 * p:2 * p + 1], w[:, 2 * p + 1:2 * p + 2])
                d = d + wt * o_ref[:, c0:c0 + LANES]
            d = _head_rms(d, g_ref[:, c0:c0 + LANES], lo).astype(BF16)
            acc = acc + jnp.dot(d, w_ref[D_NA + c0:D_NA + c0 + LANES, :], preferred_element_type=F32)
    y_ref[...] = acc


def _out_proj(a, os_, lses, gain_dil, w_bf16, x2, tm=256):
    n = x2.shape[0]
    row = lambda i: (i, 0)
    fixed = lambda i: (0, 0)
    return pl.pallas_call(
        _out_kernel,
        out_shape=jax.ShapeDtypeStruct((n, D_MODEL), F32),
        grid=(n // tm,),
        in_specs=[pl.BlockSpec((tm, D_NA), row)]
        + [pl.BlockSpec((tm, D_DIL), row)] * 3
        + [pl.BlockSpec((tm, N_DIL_GROUPS * LANES), row)] * 3
        + [pl.BlockSpec((1, D_DIL), fixed), pl.BlockSpec((D_MODEL, D_MODEL), fixed),
           pl.BlockSpec((tm, D_MODEL), row)],
        out_specs=pl.BlockSpec((tm, D_MODEL), row),
        compiler_params=_params("parallel"),
        name="out_proj",
    )(a, *os_, *lses, gain_dil.reshape(1, D_DIL), w_bf16, x2)


def _router_kernel(x_ref, g_ref, wr_ref, h_ref, aff_ref):
    xf = x_ref[...]
    h = xf * _rms_scale(xf) * g_ref[...]
    h_ref[...] = h.astype(BF16)
    logits = jnp.dot(h, wr_ref[...], preferred_element_type=F32, precision=lax.Precision.HIGHEST)
    m = jnp.max(logits, axis=-1, keepdims=True)
    e = jnp.exp(logits - m)
    aff_ref[...] = e / jnp.sum(e, axis=-1, keepdims=True)


def _router(x2, gain, w_router, tm=512):
    n = x2.shape[0]
    return pl.pallas_call(
        _router_kernel,
        out_shape=(jax.ShapeDtypeStruct((n, D_MODEL), BF16),
                   jax.ShapeDtypeStruct((n, N_EXPERTS), F32)),
        grid=(n // tm,),
        in_specs=[pl.BlockSpec((tm, D_MODEL), lambda i: (i, 0)),
                  pl.BlockSpec((1, D_MODEL), lambda i: (0, 0)),
                  pl.BlockSpec((D_MODEL, N_EXPERTS), lambda i: (0, 0))],
        out_specs=(pl.BlockSpec((tm, D_MODEL), lambda i: (i, 0)),
                   pl.BlockSpec((tm, N_EXPERTS), lambda i: (i, 0))),
        compiler_params=_params("parallel"),
        name="router",
    )(x2, gain.reshape(1, D_MODEL), w_router)


def _ffn_kernel(xe_ref, g_ref, wg_ref, wu_ref, wd_ref, ye_ref):
    xe = xe_ref[...]
    gate = jnp.dot(xe, wg_ref[...], preferred_element_type=F32)
    up = jnp.dot(xe, wu_ref[...], preferred_element_type=F32)
    hid = (gate * jax.nn.sigmoid(gate) * up).astype(BF16)
    ye_ref[...] = jnp.dot(hid, wd_ref[...], preferred_element_type=F32) * g_ref[...]


def _expert_ffn(xe, g, wg, wu, wd, tm=512):
    e, cap, d = xe.shape
    tm = min(tm, cap)
    tile = lambda i, j: (i, j, 0)
    wspec = pl.BlockSpec((None, d, d), lambda i, j: (i, 0, 0))
    return pl.pallas_call(
        _ffn_kernel,
        out_shape=jax.ShapeDtypeStruct((e, cap, d), F32),
        grid=(e, cap // tm),
        in_specs=[pl.BlockSpec((None, tm, d), tile), pl.BlockSpec((None, tm, 1), tile),
                  wspec, wspec, wspec],
        out_specs=pl.BlockSpec((None, tm, d), tile),
        compiler_params=_params("parallel", "arbitrary"),
        name="expert_ffn",
    )(xe, g.reshape(e, cap, 1), wg, wu, wd)


def _final_kernel(x_ref, g_ref, o_ref):
    xf = x_ref[...]
    o_ref[...] = xf * _rms_scale(xf) * g_ref[...]


def _final_norm(x2, gain, tm=1024):
    n = x2.shape[0]
    return pl.pallas_call(
        _final_kernel,
        out_shape=jax.ShapeDtypeStruct((n, D_MODEL), F32),
        grid=(n // tm,),
        in_specs=[pl.BlockSpec((tm, D_MODEL), lambda i: (i, 0)),
                  pl.BlockSpec((1, D_MODEL), lambda i: (0, 0))],
        out_specs=pl.BlockSpec((tm, D_MODEL), lambda i: (i, 0)),
        compiler_params=_params("parallel"),
        name="final_norm",
    )(x2, gain.reshape(1, D_MODEL))


def _mixer(x2, b, s, norm1, w_in, table, gain_na, gain_dil, w_out, slopes):
    qkv = _qkv_proj(x2, norm1, w_in).reshape(b, s, 3 * D_MODEL)
    a = _na_attention(qkv, table, gain_na).reshape(b * s, D_NA)
    os_, lses = [], []
    for _, dilation in DIL_PATTERNS:
        o, lse = _dil_attention(qkv, dilation, slopes)
        os_.append(o.reshape(b * s, D_DIL))
        lses.append(lse.reshape(b * s, N_DIL_GROUPS * LANES))
    return _out_proj(a, os_, lses, gain_dil, w_out, x2)


def _moe(x2, norm2, w_router, wg, wu, wd):
    n = x2.shape[0]
    cap = CAPACITY_FACTOR * n // N_EXPERTS
    h, aff = _router(x2, norm2, w_router)
    g, idx = lax.top_k(aff.T, cap)
    xe = h[idx]
    ye = _expert_ffn(xe, g, wg, wu, wd)
    return x2.at[idx.reshape(-1)].add(ye.reshape(-1, D_MODEL))


def _trunk(x, norm1, w_in, tables, gain_na, gain_dil, w_out, norm2, w_router, wg, wu, wd, final_norm):
    b, s, _ = x.shape
    slopes = _alibi_slopes(N_HEADS_DIL)
    x2 = x.reshape(b * s, D_MODEL)
    for l in range(DEPTH):
        x2 = _mixer(x2, b, s, norm1[l], w_in[l], tables[l], gain_na[l], gain_dil[l], w_out[l], slopes)
        x2 = _moe(x2, norm2[l], w_router[l], wg[l], wu[l], wd[l])
    return _final_norm(x2, final_norm).reshape(b, s, D_MODEL)


def kernel(x_prompt, x_sample, norm1, w_in, rpb, gain_na, gain_dil, w_out, norm2, w_router, w_gate, w_up, w_down, final_norm):
    w_in_b = w_in.astype(BF16)
    w_out_b = w_out.astype(BF16)
    wg, wu, wd = w_gate.astype(BF16), w_up.astype(BF16), w_down.astype(BF16)
    tables = jax.vmap(_na_bias_table)(rpb)
    args = (norm1, w_in_b, tables, gain_na, gain_dil, w_out_b, norm2, w_router, wg, wu, wd, final_norm)
    return (_trunk(x_prompt, *args), _trunk(x_sample, *args))
```

```python
import functools
import math

import jax
import jax.numpy as jnp
import numpy as np
from jax import lax
from jax.experimental import pallas as pl
from jax.experimental.pallas import tpu as pltpu

D_MODEL = 1024
DEPTH = 4
HEAD_DIM = 64
N_HEADS = D_MODEL // HEAD_DIM
N_HEADS_NA = N_HEADS // 4
N_HEADS_DIL = N_HEADS - N_HEADS_NA
D_NA = N_HEADS_NA * HEAD_DIM
D_DIL = N_HEADS_DIL * HEAD_DIM
GRID_W = 64
NA_ROWS = 8
NA_COLS = 16
DIL_PATTERNS = ((128, 1), (512, 4), (2048, 16))
N_EXPERTS = 16
CAPACITY_FACTOR = 2
RMS_EPS = 1e-6

LANES = 128
HEADS_PER_LANE_TILE = LANES // HEAD_DIM
DIL_GROUP = 256
N_DIL_GROUPS = D_DIL // DIL_GROUP
NEG_BIG = -1e30
VMEM_LIMIT = 56 * 1024 * 1024

BF16 = jnp.bfloat16
F32 = jnp.float32


def _alibi_slopes(n):
    def pow2(m):
        start = 2.0 ** (-8.0 / m)
        return [start ** (i + 1) for i in range(m)]
    if math.log2(n).is_integer():
        s = pow2(n)
    else:
        c = 2 ** int(math.floor(math.log2(n)))
        s = pow2(c) + pow2(2 * c)[0::2][: n - c]
    return np.asarray(s, dtype=np.float32)


def _params(*sem):
    return pltpu.CompilerParams(dimension_semantics=sem, vmem_limit_bytes=VMEM_LIMIT)


def _rms_scale(xf):
    return lax.rsqrt(jnp.mean(xf * xf, axis=-1, keepdims=True) + RMS_EPS)


def _qkv_kernel(x_ref, g_ref, w_ref, o_ref):
    xf = x_ref[...]
    h = (xf * _rms_scale(xf) * g_ref[...]).astype(BF16)
    for c in range(3):
        y = jnp.dot(h, w_ref[:, c * D_MODEL:(c + 1) * D_MODEL], preferred_element_type=F32)
        if c == 0:
            y = y * (HEAD_DIM ** -0.5)
        o_ref[:, c * D_MODEL:(c + 1) * D_MODEL] = y.astype(BF16)


def _qkv_proj(x2, gain, w_bf16, tm=512):
    n = x2.shape[0]
    return pl.pallas_call(
        _qkv_kernel,
        out_shape=jax.ShapeDtypeStruct((n, 3 * D_MODEL), BF16),
        grid=(n // tm,),
        in_specs=[
            pl.BlockSpec((tm, D_MODEL), lambda i: (i, 0)),
            pl.BlockSpec((1, D_MODEL), lambda i: (0, 0)),
            pl.BlockSpec((D_MODEL, 3 * D_MODEL), lambda i: (0, 0)),
        ],
        out_specs=pl.BlockSpec((tm, 3 * D_MODEL), lambda i: (i, 0)),
        compiler_params=_params("parallel"),
        name="qkv_proj",
    )(x2, gain.reshape(1, D_MODEL), w_bf16)


def _na_bias_table(rpb):
    qc = np.arange(GRID_W)[:, None]
    kc = np.arange(GRID_W)[None, :]
    win0 = np.clip(qc - NA_COLS // 2, 0, GRID_W - NA_COLS)
    in_win = (kc >= win0) & (kc < win0 + NA_COLS)
    ci = np.clip(kc - qc + NA_COLS - 1, 0, 2 * NA_COLS - 2)
    ri = np.arange(NA_ROWS)[:, None] + np.arange(NA_ROWS)[None, :]
    t = rpb.astype(F32)[:, ri[:, :, None, None], ci[None, None]]
    t = jnp.where(in_win[None, None, None], t, NEG_BIG)
    t = jnp.transpose(t, (0, 1, 3, 2, 4))
    return t.reshape(N_HEADS_NA, NA_ROWS, GRID_W, NA_ROWS * GRID_W)


def _head_rms(o, gain_tile, lo):
    o2 = o * o
    ms_lo = jnp.sum(jnp.where(lo, o2, 0.0), axis=-1, keepdims=True) * (1.0 / HEAD_DIM)
    ms_hi = jnp.sum(jnp.where(lo, 0.0, o2), axis=-1, keepdims=True) * (1.0 / HEAD_DIM)
    scale = jnp.where(lo, lax.rsqrt(ms_lo + RMS_EPS), lax.rsqrt(ms_hi + RMS_EPS))
    return o * scale * gain_tile


def _na_kernel(q_ref, k_ref, v_ref, t_ref, g_ref, o_ref, *, rows):
    nk = NA_ROWS * GRID_W
    lo = lax.broadcasted_iota(jnp.int32, (1, LANES), 1) < HEAD_DIM

    def row_body(r, carry):
        kr0 = jnp.clip(r - NA_ROWS // 2, 0, rows - NA_ROWS)
        var = kr0 - r + NA_ROWS - 1
        q0 = pl.multiple_of(r * GRID_W, GRID_W)
        k0 = pl.multiple_of(kr0 * GRID_W, GRID_W)
        for p in range(D_NA // LANES):
            lanes = slice(p * LANES, (p + 1) * LANES)
            q2 = q_ref[0, pl.ds(q0, GRID_W), lanes]
            k2 = k_ref[0, pl.ds(k0, nk), lanes]
            v2 = v_ref[0, pl.ds(k0, nk), lanes]
            outs = []
            for hh in range(HEADS_PER_LANE_TILE):
                h = p * HEADS_PER_LANE_TILE + hh
                qm = jnp.where(lo if hh == 0 else ~lo, q2, jnp.zeros_like(q2))
                s = lax.dot_general(qm, k2, (((1,), (1,)), ((), ())), preferred_element_type=F32)
                s = s + t_ref[h, var]
                m = jnp.max(s, axis=-1, keepdims=True)
                e = jnp.exp(s - m)
                l = jnp.sum(e, axis=-1, keepdims=True)
                o = jnp.dot(e.astype(BF16), v2, preferred_element_type=F32)
                outs.append(o / l)
            o = jnp.where(lo, outs[0], outs[1])
            o_ref[0, pl.ds(q0, GRID_W), lanes] = _head_rms(o, g_ref[:, lanes], lo).astype(BF16)
        return carry

    lax.fori_loop(0, rows, row_body, 0)


def _na_attention(qkv, table, gain_na):
    b, s, _ = qkv.shape
    rows = s // GRID_W
    nblk = D_MODEL // D_NA
    return pl.pallas_call(
        functools.partial(_na_kernel, rows=rows),
        out_shape=jax.ShapeDtypeStruct((b, s, D_NA), BF16),
        grid=(b,),
        in_specs=[
            pl.BlockSpec((1, s, D_NA), lambda i: (i, 0, 0)),
            pl.BlockSpec((1, s, D_NA), lambda i: (i, 0, nblk)),
            pl.BlockSpec((1, s, D_NA), lambda i: (i, 0, 2 * nblk)),
            pl.BlockSpec(table.shape, lambda i: (0, 0, 0, 0)),
            pl.BlockSpec((1, D_NA), lambda i: (0, 0)),
        ],
        out_specs=pl.BlockSpec((1, s, D_NA), lambda i: (i, 0, 0)),
        compiler_params=_params("parallel"),
        name="na_attention",
    )(qkv, qkv, qkv, table, gain_na.reshape(1, D_NA))


def _dil_kernel(q_ref, k_ref, v_ref, o_ref, lse_ref, *, length, dilation, slopes, tq):
    half = 64
    kw = min(length, tq + 2 * half)
    lo = lax.broadcasted_iota(jnp.int32, (1, LANES), 1) < HEAD_DIM
    lane = lax.broadcasted_iota(jnp.int32, (1, LANES), 1)
    grp = pl.program_id(2)
    qi = lax.broadcasted_iota(jnp.int32, (tq, kw), 0)
    ki = lax.broadcasted_iota(jnp.int32, (tq, kw), 1)

    def blk_body(i, carry):
        q0 = pl.multiple_of(i * tq, tq)
        k0 = pl.multiple_of(jnp.clip(q0 - half, 0, length - kw), half)
        dist = jnp.abs(qi + (q0 - k0) - ki)
        valid = dist <= half
        distf = dist.astype(F32) * float(dilation)
        lse_tile = jnp.zeros((tq, LANES), F32)
        for p in range(DIL_GROUP // LANES):
            lanes = slice(p * LANES, (p + 1) * LANES)
            q2 = q_ref[0, pl.ds(q0, tq), lanes]
            k2 = k_ref[0, pl.ds(k0, kw), lanes]
            v2 = v_ref[0, pl.ds(k0, kw), lanes]
            outs = []
            for hh in range(HEADS_PER_LANE_TILE):
                hl = p * HEADS_PER_LANE_TILE + hh
                slope = jnp.float32(slopes[hl])
                for g in range(1, N_DIL_GROUPS):
                    slope = jnp.where(grp == g, jnp.float32(slopes[g * 4 + hl]), slope)
                qm = jnp.where(lo if hh == 0 else ~lo, q2, jnp.zeros_like(q2))
                s = lax.dot_general(qm, k2, (((1,), (1,)), ((), ())), preferred_element_type=F32)
                s = jnp.where(valid, s - slope * distf, NEG_BIG)
                m = jnp.max(s, axis=-1, keepdims=True)
                e = jnp.exp(s - m)
                l = jnp.sum(e, axis=-1, keepdims=True)
                o = jnp.dot(e.astype(BF16), v2, preferred_element_type=F32)
                outs.append(o / l)
                lse_tile = jnp.where(lane == hl, m + jnp.log(l), lse_tile)
            o_ref[0, pl.ds(q0, tq), lanes] = jnp.where(lo, outs[0], outs[1])
        lse_ref[0, pl.ds(q0, tq), :] = lse_tile
        return carry

    lax.fori_loop(0, length // tq, blk_body, 0)


def _dil_attention(qkv, dilation, slopes, tq=128):
    b, s, _ = qkv.shape
    length = s // dilation
    cpt = 3 * D_MODEL // DIL_GROUP
    qoff = D_NA // DIL_GROUP
    per = D_MODEL // DIL_GROUP
    view = qkv.reshape(b, length, dilation * 3 * D_MODEL)

    def spec(part):
        return pl.BlockSpec((1, length, DIL_GROUP),
                            lambda i, r, g: (i, 0, r * cpt + part * per + qoff + g))

    o, lse = pl.pallas_call(
        functools.partial(_dil_kernel, length=length, dilation=dilation,
                          slopes=tuple(float(x) for x in slopes), tq=tq),
        out_shape=(jax.ShapeDtypeStruct((b, length, dilation * D_DIL), F32),
                   jax.ShapeDtypeStruct((b, length, dilation * N_DIL_GROUPS * LANES), F32)),
        grid=(b, dilation, N_DIL_GROUPS),
        in_specs=[spec(0), spec(1), spec(2)],
        out_specs=(pl.BlockSpec((1, length, DIL_GROUP), lambda i, r, g: (i, 0, r * N_DIL_GROUPS + g)),
                   pl.BlockSpec((1, length, LANES), lambda i, r, g: (i, 0, r * N_DIL_GROUPS + g))),
        compiler_params=_params("parallel", "parallel", "parallel"),
        name=f"dil_attention_d{dilation}",
    )(view, view, view)
    return o.reshape(b, s, D_DIL), lse.reshape(b, s, N_DIL_GROUPS * LANES)


def _out_kernel(a_ref, o1_ref, o2_ref, o3_ref, l1_ref, l2_ref, l3_ref, g_ref, w_ref, x_ref, y_ref):
    lo = lax.broadcasted_iota(jnp.int32, (1, LANES), 1) < HEAD_DIM
    acc = x_ref[...] + jnp.dot(a_ref[...], w_ref[0:D_NA, :], preferred_element_type=F32)
    for g in range(N_DIL_GROUPS):
        ls = [r[:, g * LANES:(g + 1) * LANES] for r in (l1_ref, l2_ref, l3_ref)]
        m = jnp.maximum(jnp.maximum(ls[0], ls[1]), ls[2])
        es = [jnp.exp(l - m) for l in ls]
        inv = 1.0 / (es[0] + es[1] + es[2])
        ws = [e * inv for e in es]
        for p in range(DIL_GROUP // LANES):
            c0 = g * DIL_GROUP + p * LANES
            d = jnp.zeros((a_ref.shape[0], LANES), F32)
            for w, o_ref in zip(ws, (o1_ref, o2_ref, o3_ref)):
                wt = jnp.where(lo, w[:, 2 * p:2 * p + 1], w[:, 2 * p + 1:2 * p + 2])
                d = d + wt * o_ref[:, c0:c0 + LANES]
            d = _head_rms(d, g_ref[:, c0:c0 + LANES], lo).astype(BF16)
            acc = acc + jnp.dot(d, w_ref[D_NA + c0:D_NA + c0 + LANES, :], preferred_element_type=F32)
    y_ref[...] = acc


def _out_proj(a, os_, lses, gain_dil, w_bf16, x2, tm=256):
    n = x2.shape[0]
    row = lambda i: (i, 0)
    fixed = lambda i: (0, 0)
    return pl.pallas_call(
        _out_kernel,
        out_shape=jax.ShapeDtypeStruct((n, D_MODEL), F32),
        grid=(n // tm,),
        in_specs=[pl.BlockSpec((tm, D_NA), row)]
        + [pl.BlockSpec((tm, D_DIL), row)] * 3
        + [pl.BlockSpec((tm, N_DIL_GROUPS * LANES), row)] * 3
        + [pl.BlockSpec((1, D_DIL), fixed), pl.BlockSpec((D_MODEL, D_MODEL), fixed),
           pl.BlockSpec((tm, D_MODEL), row)],
        out_specs=pl.BlockSpec((tm, D_MODEL), row),
        compiler_params=_params("parallel"),
        name="out_proj",
    )(a, *os_, *lses, gain_dil.reshape(1, D_DIL), w_bf16, x2)


MOE_TILE = 256
MOE_CHUNK = 64
HEXT = D_MODEL + LANES
GATE_PIECES = 3


def _route_kernel(x_ref, g_ref, wr_ref, hext_ref, aff_ref):
    xf = x_ref[...]
    h = xf * _rms_scale(xf) * g_ref[...]
    hext_ref[:, :D_MODEL] = h.astype(BF16)
    logits = jnp.dot(h, wr_ref[...], preferred_element_type=F32, precision=lax.Precision.HIGHEST)
    lane = lax.broadcasted_iota(jnp.int32, (1, LANES), 1)
    first = lane < N_EXPERTS
    used = lane < GATE_PIECES * N_EXPERTS
    m = jnp.max(jnp.where(first, logits, NEG_BIG), axis=-1, keepdims=True)
    e = jnp.exp(jnp.where(used, logits - m, 0.0))
    aff = e / jnp.sum(jnp.where(first, e, 0.0), axis=-1, keepdims=True)
    aff_ref[...] = aff[:, :N_EXPERTS]
    hi = aff.astype(BF16)
    r1 = aff - hi.astype(F32)
    mid = r1.astype(BF16)
    lo = (r1 - mid.astype(F32)).astype(BF16)
    pieces = jnp.where(first, hi, jnp.where(lane < 2 * N_EXPERTS, mid, lo))
    hext_ref[:, D_MODEL:] = jnp.where(used, pieces, jnp.zeros_like(pieces))


def _route(x2, gain, wr_pad, tm=512):
    n = x2.shape[0]
    return pl.pallas_call(
        _route_kernel,
        out_shape=(jax.ShapeDtypeStruct((n, HEXT), BF16),
                   jax.ShapeDtypeStruct((n, N_EXPERTS), F32)),
        grid=(n // tm,),
        in_specs=[pl.BlockSpec((tm, D_MODEL), lambda i: (i, 0)),
                  pl.BlockSpec((1, D_MODEL), lambda i: (0, 0)),
                  pl.BlockSpec((D_MODEL, LANES), lambda i: (0, 0))],
        out_specs=(pl.BlockSpec((tm, HEXT), lambda i: (i, 0)),
                   pl.BlockSpec((tm, N_EXPERTS), lambda i: (i, 0))),
        compiler_params=_params("parallel"),
        name="route",
    )(x2, gain.reshape(1, D_MODEL), wr_pad)


def _thr_kernel(aff_ref, thr_ref, need_ref, *, cap):
    bits = lax.bitcast_convert_type(aff_ref[...], jnp.int32)

    def count(mask):
        per_lane = jnp.sum(mask.astype(jnp.int32), axis=0)
        return jnp.sum(per_lane, axis=-1, keepdims=True)

    def body(i, prefix):
        cand = prefix | jnp.left_shift(jnp.int32(1), 30 - i)
        return jnp.where(count(bits >= cand[None]) >= cap, cand, prefix)

    thr = lax.fori_loop(0, 31, body, jnp.zeros((N_EXPERTS, 1), jnp.int32))
    thr_ref[...] = jnp.broadcast_to(thr, (N_EXPERTS, LANES))
    need_ref[...] = jnp.broadcast_to(cap - count(bits > thr[None]), (N_EXPERTS, LANES))


def _threshold(aff3, cap):
    out = jax.ShapeDtypeStruct((N_EXPERTS, LANES), jnp.int32)
    return pl.pallas_call(
        functools.partial(_thr_kernel, cap=cap),
        out_shape=(out, out),
        compiler_params=_params(),
        name="route_threshold",
    )(aff3)


def _pos_kernel(aff_ref, thr_ref, need_ref, tri_ref, dest_ref, start_ref, cnt_ref, run_eq, run_sel, *, tc):
    @pl.when(pl.program_id(0) == 0)
    def _():
        run_eq[...] = jnp.zeros_like(run_eq)
        run_sel[...] = jnp.zeros_like(run_sel)

    bits = lax.bitcast_convert_type(aff_ref[...], jnp.int32)
    thr = thr_ref[:, :1]
    need = need_ref[:, :1].astype(F32)
    gt = bits > thr[None]
    eq = bits == thr[None]

    def prefix(mask):
        f = mask.astype(F32)
        incl = jnp.dot(f.astype(BF16).reshape(tc * N_EXPERTS, MOE_TILE), tri_ref[...],
                       preferred_element_type=F32)
        return f, incl.reshape(tc, N_EXPERTS, MOE_TILE)

    eqf, eq_incl = prefix(eq)
    base = run_eq[:, :1]
    sels = []
    for j in range(tc):
        before = base + eq_incl[j] - eqf[j]
        sels.append(gt[j] | (eq[j] & (before < need)))
        base = base + eq_incl[j, :, MOE_TILE - 1:]
    run_eq[...] = jnp.broadcast_to(base, run_eq.shape)
    sel = jnp.stack(sels)
    _, incl = prefix(sel)
    dest_ref[...] = jnp.where(sel, incl - 1.0, -1.0).astype(jnp.int32)
    base = run_sel[:, :1]
    for j in range(tc):
        cnt = incl[j, :, MOE_TILE - 1:]
        start_ref[j] = jnp.broadcast_to(base, (N_EXPERTS, LANES)).astype(jnp.int32)
        cnt_ref[j] = jnp.broadcast_to(cnt, (N_EXPERTS, LANES)).astype(jnp.int32)
        base = base + cnt
    run_sel[...] = jnp.broadcast_to(base, run_sel.shape)


def _positions(aff3, thr, need, tc=8):
    nt = aff3.shape[0]
    tri = jnp.asarray(np.triu(np.ones((MOE_TILE, MOE_TILE), np.float32)), BF16)
    blk = lambda i: (i, 0, 0)
    fixed = lambda i: (0, 0)
    small = jax.ShapeDtypeStruct((nt, N_EXPERTS, LANES), jnp.int32)
    return pl.pallas_call(
        functools.partial(_pos_kernel, tc=tc),
        out_shape=(jax.ShapeDtypeStruct((nt, N_EXPERTS, MOE_TILE), jnp.int32), small, small),
        grid=(nt // tc,),
        in_specs=[pl.BlockSpec((tc, N_EXPERTS, MOE_TILE), blk),
                  pl.BlockSpec((N_EXPERTS, LANES), fixed), pl.BlockSpec((N_EXPERTS, LANES), fixed),
                  pl.BlockSpec((MOE_TILE, MOE_TILE), fixed)],
        out_specs=(pl.BlockSpec((tc, N_EXPERTS, MOE_TILE), blk),
                   pl.BlockSpec((tc, N_EXPERTS, LANES), blk), pl.BlockSpec((tc, N_EXPERTS, LANES), blk)),
        scratch_shapes=[pltpu.VMEM((N_EXPERTS, LANES), F32), pltpu.VMEM((N_EXPERTS, LANES), F32)],
        compiler_params=_params("arbitrary"),
        name="route_positions",
    )(aff3, thr, need, tri)


ROW_ALIGN = 16
MOE_WINDOW = MOE_CHUNK + ROW_ALIGN
MOE_STAGE_ROWS = N_EXPERTS * MOE_WINDOW


def _rounds(cnt_sm, tile):
    most = cnt_sm[tile * N_EXPERTS]
    for e in range(1, N_EXPERTS):
        most = jnp.maximum(most, cnt_sm[tile * N_EXPERTS + e])
    return (most + MOE_CHUNK - 1) // MOE_CHUNK


def _round_window(start_sm, cnt_sm, tile, e, r):
    base = jnp.minimum(r * MOE_CHUNK, cnt_sm[tile * N_EXPERTS + e])
    return base, start_sm[tile * N_EXPERTS + e] + base


def _align_down(v):
    return v & -ROW_ALIGN


def _align_rest(v):
    return v & (ROW_ALIGN - 1)


def _gather_kernel(start_sm, cnt_sm, dest_ref, hext_ref, xe_hbm, stage, sems, state, prev_row):
    i = pl.program_id(0)
    kio = lax.broadcasted_iota(jnp.int32, (MOE_WINDOW, MOE_TILE), 0)
    rid = lax.broadcasted_iota(jnp.int32, (ROW_ALIGN, HEXT), 0)

    @pl.when(i == 0)
    def _():
        state[0] = 0
        state[1] = 0
        for e in range(N_EXPERTS):
            prev_row[e] = 0
        stage[...] = jnp.zeros_like(stage)
        pads = [pltpu.make_async_copy(stage.at[0, pl.ds(0, MOE_WINDOW)],
                                      xe_hbm.at[e, pl.ds(xe_hbm.shape[1] - MOE_WINDOW, MOE_WINDOW)], sems.at[2])
                for e in range(N_EXPERTS)]
        for c in pads:
            c.start()
        for c in pads:
            c.wait()

    def copy(e, s, row):
        return pltpu.make_async_copy(stage.at[s, pl.ds(e * MOE_WINDOW, MOE_WINDOW)],
                                     xe_hbm.at[e, pl.ds(pl.multiple_of(row, ROW_ALIGN), MOE_WINDOW)], sems.at[s])

    def wait_all(s):
        for e in range(N_EXPERTS):
            copy(e, s, 0).wait()

    def batch(r):
        cur = state[0]
        new = 1 - cur
        dest = dest_ref[0]
        firsts, rows = [], []
        for e in range(N_EXPERTS):
            base, first = _round_window(start_sm, cnt_sm, i, e, r)
            local = dest[e:e + 1, :] - base
            ok = (local >= 0) & (local < MOE_CHUNK)
            rows.append(jnp.where(ok, local + _align_rest(first), -1) == kio)
            firsts.append(first)
        onehot = jnp.concatenate(rows, axis=0).astype(F32).astype(BF16)
        res = jnp.dot(onehot, hext_ref[...], preferred_element_type=F32).astype(BF16)
        for e in range(N_EXPERTS):
            aligned = _align_down(firsts[e])
            off = pl.multiple_of(aligned - prev_row[e], ROW_ALIGN)
            old = stage[cur, pl.ds(e * MOE_WINDOW + off, ROW_ALIGN), :]
            lo = e * MOE_WINDOW
            stage[new, lo:lo + ROW_ALIGN, :] = jnp.where(rid < _align_rest(firsts[e]), old, res[lo:lo + ROW_ALIGN])
            stage[new, lo + ROW_ALIGN:lo + MOE_WINDOW, :] = res[lo + ROW_ALIGN:lo + MOE_WINDOW]

        @pl.when(state[1] > 0)
        def _():
            wait_all(cur)

        for e in range(N_EXPERTS):
            aligned = _align_down(firsts[e])
            copy(e, new, aligned).start()
            prev_row[e] = aligned
        state[0] = new
        state[1] = state[1] + 1

    batch(0)

    def extra_round(r, carry):
        batch(r)
        return carry

    lax.fori_loop(1, _rounds(cnt_sm, i), extra_round, 0)

    @pl.when(i == pl.num_programs(0) - 1)
    def _():
        wait_all(state[0])


def _gather(starts, cnts, dest3, hext, cap):
    nt = dest3.shape[0]
    return pl.pallas_call(
        _gather_kernel,
        out_shape=jax.ShapeDtypeStruct((N_EXPERTS, cap + MOE_WINDOW, HEXT), BF16),
        grid_spec=pltpu.PrefetchScalarGridSpec(
            num_scalar_prefetch=2,
            grid=(nt,),
            in_specs=[pl.BlockSpec((1, N_EXPERTS, MOE_TILE), lambda i, *_: (i, 0, 0)),
                      pl.BlockSpec((MOE_TILE, HEXT), lambda i, *_: (i, 0))],
            out_specs=pl.BlockSpec(memory_space=pl.ANY),
            scratch_shapes=[pltpu.VMEM((2, MOE_STAGE_ROWS, HEXT), BF16), pltpu.SemaphoreType.DMA((3,)),
                            pltpu.SMEM((2,), jnp.int32), pltpu.SMEM((N_EXPERTS,), jnp.int32)],
        ),
        compiler_params=_params("arbitrary"),
        name="route_gather",
    )(starts, cnts, dest3, hext)


def _ffn_kernel(xe_ref, wg_ref, wu_ref, wd_ref, ye_ref):
    xe = xe_ref[:, :D_MODEL]
    pieces = xe_ref[:, D_MODEL:].astype(F32)
    lane = lax.broadcasted_iota(jnp.int32, (1, LANES), 1)
    mine = (lane % N_EXPERTS == pl.program_id(0)) & (lane < GATE_PIECES * N_EXPERTS)
    g = jnp.sum(jnp.where(mine, pieces, 0.0), axis=-1, keepdims=True)
    gate = jnp.dot(xe, wg_ref[...], preferred_element_type=F32)
    up = jnp.dot(xe, wu_ref[...], preferred_element_type=F32)
    hid = (gate * jax.nn.sigmoid(gate) * up).astype(BF16)
    ye_ref[...] = (jnp.dot(hid, wd_ref[...], preferred_element_type=F32) * g).astype(BF16)


def _expert_ffn(xe, cap, wg, wu, wd, tm=512):
    tm = min(tm, cap)
    d = D_MODEL
    tile = lambda i, j: (i, j, 0)
    wspec = pl.BlockSpec((None, d, d), lambda i, j: (i, 0, 0))
    return pl.pallas_call(
        _ffn_kernel,
        out_shape=jax.ShapeDtypeStruct((N_EXPERTS, cap, d), BF16),
        grid=(N_EXPERTS, cap // tm),
        in_specs=[pl.BlockSpec((None, tm, HEXT), tile), wspec, wspec, wspec],
        out_specs=pl.BlockSpec((None, tm, d), tile),
        compiler_params=_params("parallel", "arbitrary"),
        name="expert_ffn",
    )(xe, wg, wu, wd)


def _combine_kernel(start_sm, cnt_sm, dest_ref, x_ref, ye_hbm, out_ref, stage, sems, *, cap):
    i = pl.program_id(0)
    slot = i % 2
    lane = lax.broadcasted_iota(jnp.int32, (1, LANES), 1)
    lane_e = lax.broadcasted_iota(jnp.int32, (1, N_EXPERTS), 1)

    def window(tile, e, r):
        base, first = _round_window(start_sm, cnt_sm, tile, e, r)
        return base, first, jnp.minimum(_align_down(first), cap - MOE_WINDOW)

    def copy(e, s, row):
        return pltpu.make_async_copy(ye_hbm.at[e, pl.ds(pl.multiple_of(row, ROW_ALIGN), MOE_WINDOW)],
                                     stage.at[s, pl.ds(e * MOE_WINDOW, MOE_WINDOW)], sems.at[s])

    def fetch(tile, s, r):
        for e in range(N_EXPERTS):
            copy(e, s, window(tile, e, r)[2]).start()

    def wait_all(s):
        for e in range(N_EXPERTS):
            copy(e, s, 0).wait()

    def expand(r):
        lo = jnp.zeros((1, N_EXPERTS), jnp.int32)
        shift = jnp.zeros((1, N_EXPERTS), jnp.int32)
        for e in range(N_EXPERTS):
            base, first, row = window(i, e, r)
            lo = jnp.where(lane_e == e, base, lo)
            shift = jnp.where(lane_e == e, first - row + e * MOE_WINDOW - base, shift)
        rank = dest_ref[0]
        col = jnp.where((rank >= lo) & (rank < lo + MOE_CHUNK), rank + shift, -1)
        tiles = []
        for j in range(MOE_STAGE_ROWS // LANES):
            hit = None
            for e in range(j * LANES // MOE_WINDOW, min(N_EXPERTS - 1, ((j + 1) * LANES - 1) // MOE_WINDOW) + 1):
                m = col[:, e:e + 1] == lane + j * LANES
                hit = m if hit is None else hit | m
            tiles.append(hit)
        onehot = jnp.concatenate(tiles, axis=1).astype(F32).astype(BF16)
        return jnp.dot(onehot, stage[slot], preferred_element_type=F32)

    @pl.when(i == 0)
    def _():
        fetch(0, 0, 0)

    @pl.when(i + 1 < pl.num_programs(0))
    def _():
        fetch(i + 1, 1 - slot, 0)

    wait_all(slot)
    out_ref[...] = x_ref[...] + expand(0)

    def extra_round(r, carry):
        fetch(i, slot, r)
        wait_all(slot)
        out_ref[...] += expand(r)
        return carry

    lax.fori_loop(1, _rounds(cnt_sm, i), extra_round, 0)


def _combine(starts, cnts, dest_t, x2, ye, cap):
    nt = dest_t.shape[0]
    return pl.pallas_call(
        functools.partial(_combine_kernel, cap=cap),
        out_shape=jax.ShapeDtypeStruct(x2.shape, F32),
        grid_spec=pltpu.PrefetchScalarGridSpec(
            num_scalar_prefetch=2,
            grid=(nt,),
            in_specs=[pl.BlockSpec((1, MOE_TILE, N_EXPERTS), lambda i, *_: (i, 0, 0)),
                      pl.BlockSpec((MOE_TILE, D_MODEL), lambda i, *_: (i, 0)),
                      pl.BlockSpec(memory_space=pl.ANY)],
            out_specs=pl.BlockSpec((MOE_TILE, D_MODEL), lambda i, *_: (i, 0)),
            scratch_shapes=[pltpu.VMEM((2, MOE_STAGE_ROWS, D_MODEL), BF16), pltpu.SemaphoreType.DMA((2,))],
        ),
        compiler_params=_params("arbitrary"),
        name="route_combine",
    )(starts, cnts, dest_t, x2, ye)


def _final_kernel(x_ref, g_ref, o_ref):
    xf = x_ref[...]
    o_ref[...] = xf * _rms_scale(xf) * g_ref[...]


def _final_norm(x2, gain, tm=1024):
    n = x2.shape[0]
    return pl.pallas_call(
        _final_kernel,
        out_shape=jax.ShapeDtypeStruct((n, D_MODEL), F32),
        grid=(n // tm,),
        in_specs=[pl.BlockSpec((tm, D_MODEL), lambda i: (i, 0)),
                  pl.BlockSpec((1, D_MODEL), lambda i: (0, 0))],
        out_specs=pl.BlockSpec((tm, D_MODEL), lambda i: (i, 0)),
        compiler_params=_params("parallel"),
        name="final_norm",
    )(x2, gain.reshape(1, D_MODEL))


def _mixer(x2, b, s, norm1, w_in, table, gain_na, gain_dil, w_out, slopes):
    qkv = _qkv_proj(x2, norm1, w_in).reshape(b, s, 3 * D_MODEL)
    a = _na_attention(qkv, table, gain_na).reshape(b * s, D_NA)
    os_, lses = [], []
    for _, dilation in DIL_PATTERNS:
        o, lse = _dil_attention(qkv, dilation, slopes)
        os_.append(o.reshape(b * s, D_DIL))
        lses.append(lse.reshape(b * s, N_DIL_GROUPS * LANES))
    return _out_proj(a, os_, lses, gain_dil, w_out, x2)


def _moe(x2, norm2, wr_pad, wg, wu, wd):
    n = x2.shape[0]
    cap = CAPACITY_FACTOR * n // N_EXPERTS
    nt = n // MOE_TILE
    hext, aff = _route(x2, norm2, wr_pad)
    aff3 = aff.reshape(nt, MOE_TILE, N_EXPERTS).transpose(0, 2, 1)
    thr, need = _threshold(aff3, cap)
    dest3, start3, cnt3 = _positions(aff3, thr, need)
    starts = start3[:, :, 0].reshape(-1)
    cnts = cnt3[:, :, 0].reshape(-1)
    xe = _gather(starts, cnts, dest3, hext, cap)
    ye = _expert_ffn(xe, cap, wg, wu, wd)
    return _combine(starts, cnts, dest3.transpose(0, 2, 1), x2, ye, cap)


def _trunk(x, norm1, w_in, tables, gain_na, gain_dil, w_out, norm2, w_router, wg, wu, wd, final_norm):
    b, s, _ = x.shape
    slopes = _alibi_slopes(N_HEADS_DIL)
    x2 = x.reshape(b * s, D_MODEL)
    for l in range(DEPTH):
        x2 = _mixer(x2, b, s, norm1[l], w_in[l], tables[l], gain_na[l], gain_dil[l], w_out[l], slopes)
        x2 = _moe(x2, norm2[l], w_router[l], wg[l], wu[l], wd[l])
    return _final_norm(x2, final_norm).reshape(b, s, D_MODEL)


def kernel(x_prompt, x_sample, norm1, w_in, rpb, gain_na, gain_dil, w_out, norm2, w_router, w_gate, w_up, w_down, final_norm):
    w_in_b = w_in.astype(BF16)
    w_out_b = w_out.astype(BF16)
    wg, wu, wd = w_gate.astype(BF16), w_up.astype(BF16), w_down.astype(BF16)
    tables = jax.vmap(_na_bias_table)(rpb)
    pad = jnp.zeros((DEPTH, D_MODEL, LANES - GATE_PIECES * N_EXPERTS), F32)
    wr_pad = jnp.concatenate([w_router] * GATE_PIECES + [pad], axis=-1)
    args = (norm1, w_in_b, tables, gain_na, gain_dil, w_out_b, norm2, wr_pad, wg, wu, wd, final_norm)
    return (_trunk(x_prompt, *args), _trunk(x_sample, *args))
```

```python
import functools
import math

import jax
import jax.numpy as jnp
import numpy as np
from jax import lax
from jax.experimental import pallas as pl
from jax.experimental.pallas import tpu as pltpu

D_MODEL = 1024
DEPTH = 4
HEAD_DIM = 64
N_HEADS = D_MODEL // HEAD_DIM
N_HEADS_NA = N_HEADS // 4
N_HEADS_DIL = N_HEADS - N_HEADS_NA
D_NA = N_HEADS_NA * HEAD_DIM
D_DIL = N_HEADS_DIL * HEAD_DIM
GRID_W = 64
NA_ROWS = 8
NA_COLS = 16
DIL_PATTERNS = ((128, 1), (512, 4), (2048, 16))
N_EXPERTS = 16
CAPACITY_FACTOR = 2
RMS_EPS = 1e-6

LANES = 128
HEADS_PER_LANE_TILE = LANES // HEAD_DIM
DIL_GROUP = 256
N_DIL_GROUPS = D_DIL // DIL_GROUP
NEG_BIG = -1e30
VMEM_LIMIT = 56 * 1024 * 1024

BF16 = jnp.bfloat16
F32 = jnp.float32


def _alibi_slopes(n):
    def pow2(m):
        start = 2.0 ** (-8.0 / m)
        return [start ** (i + 1) for i in range(m)]
    if math.log2(n).is_integer():
        s = pow2(n)
    else:
        c = 2 ** int(math.floor(math.log2(n)))
        s = pow2(c) + pow2(2 * c)[0::2][: n - c]
    return np.asarray(s, dtype=np.float32)


def _params(*sem):
    return pltpu.CompilerParams(dimension_semantics=sem, vmem_limit_bytes=VMEM_LIMIT)


def _rms_scale(xf):
    return lax.rsqrt(jnp.mean(xf * xf, axis=-1, keepdims=True) + RMS_EPS)


N_DIL_SLABS = D_DIL // LANES
DILATIONS = tuple(d for _, d in DIL_PATTERNS)


def _qkv_kernel(x_ref, g_ref, w_ref, na_ref, *rest, tm):
    cls_refs, slab = rest[:-1], rest[-1]
    xf = x_ref[...]
    h = (xf * _rms_scale(xf) * g_ref[...]).astype(BF16)
    for c in range(3):
        y = jnp.dot(h, w_ref[:, c * D_MODEL:(c + 1) * D_MODEL], preferred_element_type=F32)
        if c == 0:
            y = y * (HEAD_DIM ** -0.5)
        na_ref[:, c * D_NA:(c + 1) * D_NA] = y[:, :D_NA].astype(BF16)
        for j in range(N_DIL_SLABS):
            slab[j] = y[:, D_NA + j * LANES:D_NA + (j + 1) * LANES]
        for ref, dil in zip(cls_refs, DILATIONS):
            for j in range(N_DIL_SLABS):
                for r in range(dil):
                    rows = slab[j] if dil == 1 else slab.at[j][pl.ds(r, tm // dil, stride=dil), :]
                    ref[0, c * N_DIL_SLABS + j, r] = rows.astype(BF16)


def _qkv_proj(x2, b, s, gain, w_bf16, tm=512):
    n = x2.shape[0]
    tpb = s // tm
    cls_shapes = [jax.ShapeDtypeStruct((b, 3 * N_DIL_SLABS, d, s // d, LANES), BF16) for d in DILATIONS]
    cls_specs = [pl.BlockSpec((1, 3 * N_DIL_SLABS, d, tm // d, LANES), lambda i: (i // tpb, 0, 0, i % tpb, 0))
                 for d in DILATIONS]
    return pl.pallas_call(
        functools.partial(_qkv_kernel, tm=tm),
        out_shape=[jax.ShapeDtypeStruct((n, 3 * D_NA), BF16)] + cls_shapes,
        grid=(n // tm,),
        in_specs=[
            pl.BlockSpec((tm, D_MODEL), lambda i: (i, 0)),
            pl.BlockSpec((1, D_MODEL), lambda i: (0, 0)),
            pl.BlockSpec((D_MODEL, 3 * D_MODEL), lambda i: (0, 0)),
        ],
        out_specs=[pl.BlockSpec((tm, 3 * D_NA), lambda i: (i, 0))] + cls_specs,
        scratch_shapes=[pltpu.VMEM((N_DIL_SLABS, tm, LANES), F32)],
        compiler_params=_params("parallel"),
        name="qkv_proj",
    )(x2, gain.reshape(1, D_MODEL), w_bf16)


def _na_bias_table(rpb):
    qc = np.arange(GRID_W)[:, None]
    kc = np.arange(GRID_W)[None, :]
    win0 = np.clip(qc - NA_COLS // 2, 0, GRID_W - NA_COLS)
    in_win = (kc >= win0) & (kc < win0 + NA_COLS)
    ci = np.clip(kc - qc + NA_COLS - 1, 0, 2 * NA_COLS - 2)
    n_ri, n_ci = 2 * NA_ROWS - 1, 2 * NA_COLS - 1
    pick = (ci.reshape(1, -1) == np.arange(n_ci)[:, None]).astype(np.float32)
    m = jnp.dot(rpb.astype(F32).reshape(N_HEADS_NA * n_ri, n_ci), pick, precision=lax.Precision.HIGHEST)
    m = jnp.where(in_win[None, None], m.reshape(N_HEADS_NA, n_ri, GRID_W, GRID_W), NEG_BIG)
    t = jnp.stack([m[:, v:v + NA_ROWS] for v in range(NA_ROWS)], axis=1)
    t = jnp.transpose(t, (0, 1, 3, 2, 4))
    t = t.reshape(N_HEADS_NA // HEADS_PER_LANE_TILE, HEADS_PER_LANE_TILE, NA_ROWS, GRID_W, NA_ROWS * GRID_W)
    return jnp.transpose(t, (0, 2, 1, 3, 4)).reshape(
        N_HEADS_NA // HEADS_PER_LANE_TILE, NA_ROWS, HEADS_PER_LANE_TILE * GRID_W, NA_ROWS * GRID_W)


def _head_rms(o, gain_tile, lo):
    o2 = o * o
    ms_lo = jnp.sum(jnp.where(lo, o2, 0.0), axis=-1, keepdims=True) * (1.0 / HEAD_DIM)
    ms_hi = jnp.sum(jnp.where(lo, 0.0, o2), axis=-1, keepdims=True) * (1.0 / HEAD_DIM)
    scale = jnp.where(lo, lax.rsqrt(ms_lo + RMS_EPS), lax.rsqrt(ms_hi + RMS_EPS))
    return o * scale * gain_tile


def _na_kernel(q_ref, k_ref, v_ref, t_ref, g_ref, o_ref, *, rows):
    nk = NA_ROWS * GRID_W
    lo = lax.broadcasted_iota(jnp.int32, (1, LANES), 1) < HEAD_DIM

    n_slabs = D_NA // LANES
    per_trip = 2

    def row_body(rr, carry):
        where, scores, values = [], [], []
        for u in range(per_trip):
            r = rr * per_trip + u
            kr0 = jnp.clip(r - NA_ROWS // 2, 0, rows - NA_ROWS)
            q0 = pl.multiple_of(r * GRID_W, GRID_W)
            k0 = pl.multiple_of(kr0 * GRID_W, GRID_W)
            where.append((q0, kr0 - r + NA_ROWS - 1))
            for p in range(n_slabs):
                lanes = slice(p * LANES, (p + 1) * LANES)
                q2 = q_ref[0, pl.ds(q0, GRID_W), lanes]
                zero = jnp.zeros_like(q2)
                q_both = jnp.concatenate([jnp.where(lo, q2, zero), jnp.where(lo, zero, q2)], axis=0)
                scores.append(lax.dot_general(q_both, k_ref[0, pl.ds(k0, nk), lanes], (((1,), (1,)), ((), ())),
                                              preferred_element_type=F32))
                values.append(v_ref[0, pl.ds(k0, nk), lanes])
        probs, sums = [], []
        for u in range(per_trip):
            for p in range(n_slabs):
                s = scores[u * n_slabs + p] + t_ref[p, where[u][1]]
                e = jnp.exp(s - jnp.max(s, axis=-1, keepdims=True))
                sums.append(jnp.sum(e, axis=-1, keepdims=True))
                probs.append(e.astype(BF16))
        for u in range(per_trip):
            for p in range(n_slabs):
                lanes = slice(p * LANES, (p + 1) * LANES)
                o = jnp.dot(probs[u * n_slabs + p], values[u * n_slabs + p],
                            preferred_element_type=F32) / sums[u * n_slabs + p]
                o = jnp.where(lo, o[:GRID_W], o[GRID_W:])
                o_ref[0, pl.ds(where[u][0], GRID_W), lanes] = _head_rms(o, g_ref[:, lanes], lo).astype(BF16)
        return carry

    lax.fori_loop(0, rows // per_trip, row_body, 0)


def _na_attention(qkv, table, gain_na):
    b, s, _ = qkv.shape
    rows = s // GRID_W
    return pl.pallas_call(
        functools.partial(_na_kernel, rows=rows),
        out_shape=jax.ShapeDtypeStruct((b, s, D_NA), BF16),
        grid=(b,),
        in_specs=[
            pl.BlockSpec((1, s, D_NA), lambda i: (i, 0, 0)),
            pl.BlockSpec((1, s, D_NA), lambda i: (i, 0, 1)),
            pl.BlockSpec((1, s, D_NA), lambda i: (i, 0, 2)),
            pl.BlockSpec(table.shape, lambda i: (0, 0, 0, 0)),
            pl.BlockSpec((1, D_NA), lambda i: (0, 0)),
        ],
        out_specs=pl.BlockSpec((1, s, D_NA), lambda i: (i, 0, 0)),
        compiler_params=_params("parallel"),
        name="na_attention",
    )(qkv, qkv, qkv, table, gain_na.reshape(1, D_NA))


DIL_HALF = 64
SLABS_PER_GROUP = DIL_GROUP // LANES
HEADS_PER_GROUP = DIL_GROUP // HEAD_DIM


def _dil_kernel(q_ref, k_ref, v_ref, o_ref, lse_ref, bias, *, length, dilation, slopes, tq):
    kw = min(length, tq + 2 * DIL_HALF)
    n_off = 1 if length <= tq else 3
    lo = lax.broadcasted_iota(jnp.int32, (1, LANES), 1) < HEAD_DIM
    lane = lax.broadcasted_iota(jnp.int32, (1, LANES), 1)
    grp = pl.program_id(1)
    qi = lax.broadcasted_iota(jnp.int32, (tq, kw), 0)
    ki = lax.broadcasted_iota(jnp.int32, (tq, kw), 1)

    for v in range(n_off):
        dist = jnp.abs(qi + v * DIL_HALF - ki)
        reach = dist <= DIL_HALF
        distf = dist.astype(F32) * float(dilation)
        for hl in range(HEADS_PER_GROUP):
            slope = jnp.float32(slopes[hl])
            for g in range(1, N_DIL_GROUPS):
                slope = jnp.where(grp == g, jnp.float32(slopes[g * HEADS_PER_GROUP + hl]), slope)
            p, hh = divmod(hl, HEADS_PER_LANE_TILE)
            bias[v, p, hh * tq:(hh + 1) * tq, :] = jnp.where(reach, -slope * distf, NEG_BIG)

    n_blk = dilation * (length // tq)
    per_trip = 4 if n_blk % 4 == 0 else 1

    def blk_body(tt, carry):
        where, scores, values = [], [], []
        for u in range(per_trip):
            t = tt * per_trip + u
            r = t // (length // tq)
            q0 = pl.multiple_of((t % (length // tq)) * tq, tq)
            k0 = pl.multiple_of(jnp.clip(q0 - DIL_HALF, 0, length - kw), DIL_HALF)
            where.append((r, q0, (q0 - k0) // DIL_HALF))
            for p in range(SLABS_PER_GROUP):
                q2 = q_ref[0, p, r, pl.ds(q0, tq), :]
                k2 = k_ref[0, p, r, pl.ds(k0, kw), :]
                zero = jnp.zeros_like(q2)
                q_both = jnp.concatenate([jnp.where(lo, q2, zero), jnp.where(lo, zero, q2)], axis=0)
                scores.append(lax.dot_general(q_both, k2, (((1,), (1,)), ((), ())), preferred_element_type=F32))
                values.append(v_ref[0, p, r, pl.ds(k0, kw), :])
        probs, stats = [], []
        for u in range(per_trip):
            for p in range(SLABS_PER_GROUP):
                s = scores[u * SLABS_PER_GROUP + p] + bias[where[u][2], p]
                m = jnp.max(s, axis=-1, keepdims=True)
                e = jnp.exp(s - m)
                stats.append((m, jnp.sum(e, axis=-1, keepdims=True)))
                probs.append(e.astype(BF16))
        for u in range(per_trip):
            r, q0, _ = where[u]
            lse_tile = jnp.zeros((tq, LANES), F32)
            for p in range(SLABS_PER_GROUP):
                m, l = stats[u * SLABS_PER_GROUP + p]
                o = jnp.dot(probs[u * SLABS_PER_GROUP + p], values[u * SLABS_PER_GROUP + p],
                            preferred_element_type=F32) / l
                o_ref[0, p, r, pl.ds(q0, tq), :] = jnp.where(lo, o[:tq], o[tq:])
                lse = m + jnp.log(l)
                lse_tile = jnp.where(lane == 2 * p, lse[:tq], jnp.where(lane == 2 * p + 1, lse[tq:], lse_tile))
            lse_ref[0, 0, r, pl.ds(q0, tq), :] = lse_tile
        return carry

    lax.fori_loop(0, n_blk // per_trip, blk_body, 0)


def _dil_attention(cls, dilation, slopes, tq=128):
    b, _, _, length, _ = cls.shape

    def spec(part):
        return pl.BlockSpec((1, SLABS_PER_GROUP, dilation, length, LANES),
                            lambda i, g: (i, part * N_DIL_GROUPS + g, 0, 0, 0))

    n_off = 1 if length <= tq else 3
    return pl.pallas_call(
        functools.partial(_dil_kernel, length=length, dilation=dilation,
                          slopes=tuple(float(x) for x in slopes), tq=tq),
        out_shape=(jax.ShapeDtypeStruct((b, N_DIL_SLABS, dilation, length, LANES), F32),
                   jax.ShapeDtypeStruct((b, N_DIL_GROUPS, dilation, length, LANES), F32)),
        grid=(b, N_DIL_GROUPS),
        in_specs=[spec(0), spec(1), spec(2)],
        out_specs=(pl.BlockSpec((1, SLABS_PER_GROUP, dilation, length, LANES), lambda i, g: (i, g, 0, 0, 0)),
                   pl.BlockSpec((1, 1, dilation, length, LANES), lambda i, g: (i, g, 0, 0, 0))),
        scratch_shapes=[pltpu.VMEM((n_off, SLABS_PER_GROUP, HEADS_PER_LANE_TILE * tq,
                                    min(length, tq + 2 * DIL_HALF)), F32)],
        compiler_params=_params("parallel", "parallel"),
        name=f"dil_attention_d{dilation}",
    )(cls, cls, cls)


def _out_kernel(a_ref, *rest, tm):
    n_pat = len(DILATIONS)
    o_refs, l_refs = rest[:n_pat], rest[n_pat:2 * n_pat]
    g_ref, w_ref, x_ref, y_ref, nat_o, nat_l = rest[2 * n_pat:]
    lo = lax.broadcasted_iota(jnp.int32, (1, LANES), 1) < HEAD_DIM

    for k, dil in enumerate(DILATIONS):
        if dil == 1:
            continue
        for r in range(dil):
            for j in range(N_DIL_SLABS):
                nat_o.at[k, j][pl.ds(r, tm // dil, stride=dil), :] = o_refs[k][0, j, r]
            for g in range(N_DIL_GROUPS):
                nat_l.at[k, g][pl.ds(r, tm // dil, stride=dil), :] = l_refs[k][0, g, r]

    def token_rows(refs, nat, k, j):
        return refs[k][0, j, 0] if DILATIONS[k] == 1 else nat[k, j]

    acc = x_ref[...] + jnp.dot(a_ref[...], w_ref[0:D_NA, :], preferred_element_type=F32)
    for g in range(N_DIL_GROUPS):
        ls = [token_rows(l_refs, nat_l, k, g) for k in range(n_pat)]
        m = functools.reduce(jnp.maximum, ls)
        es = [jnp.exp(l - m) for l in ls]
        inv = 1.0 / functools.reduce(lambda u, v: u + v, es)
        ws = [e * inv for e in es]
        for p in range(SLABS_PER_GROUP):
            j = g * SLABS_PER_GROUP + p
            d = jnp.zeros((tm, LANES), F32)
            for k, w in enumerate(ws):
                wt = jnp.where(lo, w[:, 2 * p:2 * p + 1], w[:, 2 * p + 1:2 * p + 2])
                d = d + wt * token_rows(o_refs, nat_o, k, j)
            d = _head_rms(d, g_ref[:, j * LANES:(j + 1) * LANES], lo).astype(BF16)
            acc = acc + jnp.dot(d, w_ref[D_NA + j * LANES:D_NA + (j + 1) * LANES, :], preferred_element_type=F32)
    y_ref[...] = acc


def _out_proj(a, os_, lses, s, gain_dil, w_bf16, x2, tm=256):
    n = x2.shape[0]
    tpb = s // tm
    row = lambda i: (i, 0)
    fixed = lambda i: (0, 0)
    cls = lambda i: (i // tpb, 0, 0, i % tpb, 0)
    return pl.pallas_call(
        functools.partial(_out_kernel, tm=tm),
        out_shape=jax.ShapeDtypeStruct((n, D_MODEL), F32),
        grid=(n // tm,),
        in_specs=[pl.BlockSpec((tm, D_NA), row)]
        + [pl.BlockSpec((1, N_DIL_SLABS, d, tm // d, LANES), cls) for d in DILATIONS]
        + [pl.BlockSpec((1, N_DIL_GROUPS, d, tm // d, LANES), cls) for d in DILATIONS]
        + [pl.BlockSpec((1, D_DIL), fixed), pl.BlockSpec((D_MODEL, D_MODEL), fixed),
           pl.BlockSpec((tm, D_MODEL), row)],
        out_specs=pl.BlockSpec((tm, D_MODEL), row),
        scratch_shapes=[pltpu.VMEM((len(DILATIONS), N_DIL_SLABS, tm, LANES), F32),
                        pltpu.VMEM((len(DILATIONS), N_DIL_GROUPS, tm, LANES), F32)],
        compiler_params=_params("parallel"),
        name="out_proj",
    )(a, *os_, *lses, gain_dil.reshape(1, D_DIL), w_bf16, x2)


MOE_TILE = 256
MOE_CHUNK = 64
HEXT = D_MODEL + LANES
GATE_PIECES = 3


def _route_kernel(x_ref, g_ref, wr_ref, hext_ref, aff_ref):
    xf = x_ref[...]
    h = xf * _rms_scale(xf) * g_ref[...]
    hext_ref[:, :D_MODEL] = h.astype(BF16)
    logits = jnp.dot(h, wr_ref[...], preferred_element_type=F32, precision=lax.Precision.HIGHEST)
    lane = lax.broadcasted_iota(jnp.int32, (1, LANES), 1)
    first = lane < N_EXPERTS
    used = lane < GATE_PIECES * N_EXPERTS
    m = jnp.max(jnp.where(first, logits, NEG_BIG), axis=-1, keepdims=True)
    e = jnp.exp(jnp.where(used, logits - m, 0.0))
    aff = e / jnp.sum(jnp.where(first, e, 0.0), axis=-1, keepdims=True)
    aff_ref[...] = aff[:, :N_EXPERTS]
    hi = aff.astype(BF16)
    r1 = aff - hi.astype(F32)
    mid = r1.astype(BF16)
    lo = (r1 - mid.astype(F32)).astype(BF16)
    pieces = jnp.where(first, hi, jnp.where(lane < 2 * N_EXPERTS, mid, lo))
    hext_ref[:, D_MODEL:] = jnp.where(used, pieces, jnp.zeros_like(pieces))


def _route(x2, gain, wr_pad, tm=512):
    n = x2.shape[0]
    return pl.pallas_call(
        _route_kernel,
        out_shape=(jax.ShapeDtypeStruct((n, HEXT), BF16),
                   jax.ShapeDtypeStruct((n, N_EXPERTS), F32)),
        grid=(n // tm,),
        in_specs=[pl.BlockSpec((tm, D_MODEL), lambda i: (i, 0)),
                  pl.BlockSpec((1, D_MODEL), lambda i: (0, 0)),
                  pl.BlockSpec((D_MODEL, LANES), lambda i: (0, 0))],
        out_specs=(pl.BlockSpec((tm, HEXT), lambda i: (i, 0)),
                   pl.BlockSpec((tm, N_EXPERTS), lambda i: (i, 0))),
        compiler_params=_params("parallel"),
        name="route",
    )(x2, gain.reshape(1, D_MODEL), wr_pad)


def _thr_kernel(aff_ref, thr_ref, need_ref, *, cap):
    bits = lax.bitcast_convert_type(aff_ref[...], jnp.int32)

    def count(mask):
        per_lane = jnp.sum(mask.astype(jnp.int32), axis=0)
        return jnp.sum(per_lane, axis=-1, keepdims=True)

    def body(i, prefix):
        cand = prefix | jnp.left_shift(jnp.int32(1), 30 - i)
        return jnp.where(count(bits >= cand[None]) >= cap, cand, prefix)

    thr = lax.fori_loop(0, 31, body, jnp.zeros((N_EXPERTS, 1), jnp.int32))
    thr_ref[...] = jnp.broadcast_to(thr, (N_EXPERTS, LANES))
    need_ref[...] = jnp.broadcast_to(cap - count(bits > thr[None]), (N_EXPERTS, LANES))


def _threshold(aff3, cap):
    out = jax.ShapeDtypeStruct((N_EXPERTS, LANES), jnp.int32)
    return pl.pallas_call(
        functools.partial(_thr_kernel, cap=cap),
        out_shape=(out, out),
        compiler_params=_params(),
        name="route_threshold",
    )(aff3)


def _pos_kernel(aff_ref, thr_ref, need_ref, tri_ref, dest_ref, start_ref, cnt_ref, run_eq, run_sel, *, tc):
    @pl.when(pl.program_id(0) == 0)
    def _():
        run_eq[...] = jnp.zeros_like(run_eq)
        run_sel[...] = jnp.zeros_like(run_sel)

    bits = lax.bitcast_convert_type(aff_ref[...], jnp.int32)
    thr = thr_ref[:, :1]
    need = need_ref[:, :1].astype(F32)
    gt = bits > thr[None]
    eq = bits == thr[None]

    def prefix(mask):
        f = mask.astype(F32)
        incl = jnp.dot(f.astype(BF16).reshape(tc * N_EXPERTS, MOE_TILE), tri_ref[...],
                       preferred_element_type=F32)
        return f, incl.reshape(tc, N_EXPERTS, MOE_TILE)

    eqf, eq_incl = prefix(eq)
    base = run_eq[:, :1]
    sels = []
    for j in range(tc):
        before = base + eq_incl[j] - eqf[j]
        sels.append(gt[j] | (eq[j] & (before < need)))
        base = base + eq_incl[j, :, MOE_TILE - 1:]
    run_eq[...] = jnp.broadcast_to(base, run_eq.shape)
    sel = jnp.stack(sels)
    _, incl = prefix(sel)
    dest_ref[...] = jnp.where(sel, incl - 1.0, -1.0).astype(jnp.int32)
    base = run_sel[:, :1]
    for j in range(tc):
        cnt = incl[j, :, MOE_TILE - 1:]
        start_ref[j] = jnp.broadcast_to(base, (N_EXPERTS, LANES)).astype(jnp.int32)
        cnt_ref[j] = jnp.broadcast_to(cnt, (N_EXPERTS, LANES)).astype(jnp.int32)
        base = base + cnt
    run_sel[...] = jnp.broadcast_to(base, run_sel.shape)


def _positions(aff3, thr, need, tc=8):
    nt = aff3.shape[0]
    tri = jnp.asarray(np.triu(np.ones((MOE_TILE, MOE_TILE), np.float32)), BF16)
    blk = lambda i: (i, 0, 0)
    fixed = lambda i: (0, 0)
    small = jax.ShapeDtypeStruct((nt, N_EXPERTS, LANES), jnp.int32)
    return pl.pallas_call(
        functools.partial(_pos_kernel, tc=tc),
        out_shape=(jax.ShapeDtypeStruct((nt, N_EXPERTS, MOE_TILE), jnp.int32), small, small),
        grid=(nt // tc,),
        in_specs=[pl.BlockSpec((tc, N_EXPERTS, MOE_TILE), blk),
                  pl.BlockSpec((N_EXPERTS, LANES), fixed), pl.BlockSpec((N_EXPERTS, LANES), fixed),
                  pl.BlockSpec((MOE_TILE, MOE_TILE), fixed)],
        out_specs=(pl.BlockSpec((tc, N_EXPERTS, MOE_TILE), blk),
                   pl.BlockSpec((tc, N_EXPERTS, LANES), blk), pl.BlockSpec((tc, N_EXPERTS, LANES), blk)),
        scratch_shapes=[pltpu.VMEM((N_EXPERTS, LANES), F32), pltpu.VMEM((N_EXPERTS, LANES), F32)],
        compiler_params=_params("arbitrary"),
        name="route_positions",
    )(aff3, thr, need, tri)


ROW_ALIGN = 16
MOE_WINDOW = MOE_CHUNK + ROW_ALIGN
MOE_STAGE_ROWS = N_EXPERTS * MOE_WINDOW


def _rounds(cnt_sm, tile):
    most = cnt_sm[tile * N_EXPERTS]
    for e in range(1, N_EXPERTS):
        most = jnp.maximum(most, cnt_sm[tile * N_EXPERTS + e])
    return (most + MOE_CHUNK - 1) // MOE_CHUNK


def _round_window(start_sm, cnt_sm, tile, e, r):
    base = jnp.minimum(r * MOE_CHUNK, cnt_sm[tile * N_EXPERTS + e])
    return base, start_sm[tile * N_EXPERTS + e] + base


def _align_down(v):
    return v & -ROW_ALIGN


def _align_rest(v):
    return v & (ROW_ALIGN - 1)


def _gather_kernel(start_sm, cnt_sm, dest_ref, hext_ref, xe_hbm, stage, sems, state, prev_row):
    i = pl.program_id(0)
    kio = lax.broadcasted_iota(jnp.int32, (MOE_WINDOW, MOE_TILE), 0)
    rid = lax.broadcasted_iota(jnp.int32, (ROW_ALIGN, HEXT), 0)

    @pl.when(i == 0)
    def _():
        state[0] = 0
        state[1] = 0
        for e in range(N_EXPERTS):
            prev_row[e] = 0
        stage[...] = jnp.zeros_like(stage)
        pads = [pltpu.make_async_copy(stage.at[0, pl.ds(0, MOE_WINDOW)],
                                      xe_hbm.at[e, pl.ds(xe_hbm.shape[1] - MOE_WINDOW, MOE_WINDOW)], sems.at[2])
                for e in range(N_EXPERTS)]
        for c in pads:
            c.start()
        for c in pads:
            c.wait()

    def copy(e, s, row):
        return pltpu.make_async_copy(stage.at[s, pl.ds(e * MOE_WINDOW, MOE_WINDOW)],
                                     xe_hbm.at[e, pl.ds(pl.multiple_of(row, ROW_ALIGN), MOE_WINDOW)], sems.at[s])

    def wait_all(s):
        for e in range(N_EXPERTS):
            copy(e, s, 0).wait()

    def batch(r):
        cur = state[0]
        new = 1 - cur
        dest = dest_ref[0]
        firsts, rows = [], []
        for e in range(N_EXPERTS):
            base, first = _round_window(start_sm, cnt_sm, i, e, r)
            local = dest[e:e + 1, :] - base
            ok = (local >= 0) & (local < MOE_CHUNK)
            rows.append(jnp.where(ok, local + _align_rest(first), -1) == kio)
            firsts.append(first)
        onehot = jnp.concatenate(rows, axis=0).astype(F32).astype(BF16)
        res = jnp.dot(onehot, hext_ref[...], preferred_element_type=F32).astype(BF16)
        for e in range(N_EXPERTS):
            aligned = _align_down(firsts[e])
            off = pl.multiple_of(aligned - prev_row[e], ROW_ALIGN)
            old = stage[cur, pl.ds(e * MOE_WINDOW + off, ROW_ALIGN), :]
            lo = e * MOE_WINDOW
            stage[new, lo:lo + ROW_ALIGN, :] = jnp.where(rid < _align_rest(firsts[e]), old, res[lo:lo + ROW_ALIGN])
            stage[new, lo + ROW_ALIGN:lo + MOE_WINDOW, :] = res[lo + ROW_ALIGN:lo + MOE_WINDOW]

        @pl.when(state[1] > 0)
        def _():
            wait_all(cur)

        for e in range(N_EXPERTS):
            aligned = _align_down(firsts[e])
            copy(e, new, aligned).start()
            prev_row[e] = aligned
        state[0] = new
        state[1] = state[1] + 1

    batch(0)

    def extra_round(r, carry):
        batch(r)
        return carry

    lax.fori_loop(1, _rounds(cnt_sm, i), extra_round, 0)

    @pl.when(i == pl.num_programs(0) - 1)
    def _():
        wait_all(state[0])


def _gather(starts, cnts, dest3, hext, cap):
    nt = dest3.shape[0]
    return pl.pallas_call(
        _gather_kernel,
        out_shape=jax.ShapeDtypeStruct((N_EXPERTS, cap + MOE_WINDOW, HEXT), BF16),
        grid_spec=pltpu.PrefetchScalarGridSpec(
            num_scalar_prefetch=2,
            grid=(nt,),
            in_specs=[pl.BlockSpec((1, N_EXPERTS, MOE_TILE), lambda i, *_: (i, 0, 0)),
                      pl.BlockSpec((MOE_TILE, HEXT), lambda i, *_: (i, 0))],
            out_specs=pl.BlockSpec(memory_space=pl.ANY),
            scratch_shapes=[pltpu.VMEM((2, MOE_STAGE_ROWS, HEXT), BF16), pltpu.SemaphoreType.DMA((3,)),
                            pltpu.SMEM((2,), jnp.int32), pltpu.SMEM((N_EXPERTS,), jnp.int32)],
        ),
        compiler_params=_params("arbitrary"),
        name="route_gather",
    )(starts, cnts, dest3, hext)


def _ffn_kernel(xe_ref, wg_ref, wu_ref, wd_ref, ye_ref):
    xe = xe_ref[:, :D_MODEL]
    pieces = xe_ref[:, D_MODEL:].astype(F32)
    lane = lax.broadcasted_iota(jnp.int32, (1, LANES), 1)
    mine = (lane % N_EXPERTS == pl.program_id(0)) & (lane < GATE_PIECES * N_EXPERTS)
    g = jnp.sum(jnp.where(mine, pieces, 0.0), axis=-1, keepdims=True)
    gate = jnp.dot(xe, wg_ref[...], preferred_element_type=F32)
    up = jnp.dot(xe, wu_ref[...], preferred_element_type=F32)
    hid = (gate * jax.nn.sigmoid(gate) * up).astype(BF16)
    ye_ref[...] = (jnp.dot(hid, wd_ref[...], preferred_element_type=F32) * g).astype(BF16)


def _expert_ffn(xe, cap, wg, wu, wd, tm=512):
    tm = min(tm, cap)
    d = D_MODEL
    tile = lambda i, j: (i, j, 0)
    wspec = pl.BlockSpec((None, d, d), lambda i, j: (i, 0, 0))
    return pl.pallas_call(
        _ffn_kernel,
        out_shape=jax.ShapeDtypeStruct((N_EXPERTS, cap, d), BF16),
        grid=(N_EXPERTS, cap // tm),
        in_specs=[pl.BlockSpec((None, tm, HEXT), tile), wspec, wspec, wspec],
        out_specs=pl.BlockSpec((None, tm, d), tile),
        compiler_params=_params("parallel", "arbitrary"),
        name="expert_ffn",
    )(xe, wg, wu, wd)


def _combine_kernel(start_sm, cnt_sm, dest_ref, x_ref, ye_hbm, out_ref, stage, sems, *, cap):
    i = pl.program_id(0)
    slot = i % 2
    lane = lax.broadcasted_iota(jnp.int32, (1, LANES), 1)
    lane_e = lax.broadcasted_iota(jnp.int32, (1, N_EXPERTS), 1)

    def window(tile, e, r):
        base, first = _round_window(start_sm, cnt_sm, tile, e, r)
        return base, first, jnp.minimum(_align_down(first), cap - MOE_WINDOW)

    def copy(e, s, row):
        return pltpu.make_async_copy(ye_hbm.at[e, pl.ds(pl.multiple_of(row, ROW_ALIGN), MOE_WINDOW)],
                                     stage.at[s, pl.ds(e * MOE_WINDOW, MOE_WINDOW)], sems.at[s])

    def fetch(tile, s, r):
        for e in range(N_EXPERTS):
            copy(e, s, window(tile, e, r)[2]).start()

    def wait_all(s):
        for e in range(N_EXPERTS):
            copy(e, s, 0).wait()

    def expand(r):
        lo = jnp.zeros((1, N_EXPERTS), jnp.int32)
        shift = jnp.zeros((1, N_EXPERTS), jnp.int32)
        for e in range(N_EXPERTS):
            base, first, row = window(i, e, r)
            lo = jnp.where(lane_e == e, base, lo)
            shift = jnp.where(lane_e == e, first - row + e * MOE_WINDOW - base, shift)
        rank = dest_ref[0]
        col = jnp.where((rank >= lo) & (rank < lo + MOE_CHUNK), rank + shift, -1)
        tiles = []
        for j in range(MOE_STAGE_ROWS // LANES):
            hit = None
            for e in range(j * LANES // MOE_WINDOW, min(N_EXPERTS - 1, ((j + 1) * LANES - 1) // MOE_WINDOW) + 1):
                m = col[:, e:e + 1] == lane + j * LANES
                hit = m if hit is None else hit | m
            tiles.append(hit)
        onehot = jnp.concatenate(tiles, axis=1).astype(F32).astype(BF16)
        return jnp.dot(onehot, stage[slot], preferred_element_type=F32)

    @pl.when(i == 0)
    def _():
        fetch(0, 0, 0)

    @pl.when(i + 1 < pl.num_programs(0))
    def _():
        fetch(i + 1, 1 - slot, 0)

    wait_all(slot)
    out_ref[...] = x_ref[...] + expand(0)

    def extra_round(r, carry):
        fetch(i, slot, r)
        wait_all(slot)
        out_ref[...] += expand(r)
        return carry

    lax.fori_loop(1, _rounds(cnt_sm, i), extra_round, 0)


def _combine(starts, cnts, dest_t, x2, ye, cap):
    nt = dest_t.shape[0]
    return pl.pallas_call(
        functools.partial(_combine_kernel, cap=cap),
        out_shape=jax.ShapeDtypeStruct(x2.shape, F32),
        grid_spec=pltpu.PrefetchScalarGridSpec(
            num_scalar_prefetch=2,
            grid=(nt,),
            in_specs=[pl.BlockSpec((1, MOE_TILE, N_EXPERTS), lambda i, *_: (i, 0, 0)),
                      pl.BlockSpec((MOE_TILE, D_MODEL), lambda i, *_: (i, 0)),
                      pl.BlockSpec(memory_space=pl.ANY)],
            out_specs=pl.BlockSpec((MOE_TILE, D_MODEL), lambda i, *_: (i, 0)),
            scratch_shapes=[pltpu.VMEM((2, MOE_STAGE_ROWS, D_MODEL), BF16), pltpu.SemaphoreType.DMA((2,))],
        ),
        compiler_params=_params("arbitrary"),
        name="route_combine",
    )(starts, cnts, dest_t, x2, ye)


def _final_kernel(x_ref, g_ref, o_ref):
    xf = x_ref[...]
    o_ref[...] = xf * _rms_scale(xf) * g_ref[...]


def _final_norm(x2, gain, tm=1024):
    n = x2.shape[0]
    return pl.pallas_call(
        _final_kernel,
        out_shape=jax.ShapeDtypeStruct((n, D_MODEL), F32),
        grid=(n // tm,),
        in_specs=[pl.BlockSpec((tm, D_MODEL), lambda i: (i, 0)),
                  pl.BlockSpec((1, D_MODEL), lambda i: (0, 0))],
        out_specs=pl.BlockSpec((tm, D_MODEL), lambda i: (i, 0)),
        compiler_params=_params("parallel"),
        name="final_norm",
    )(x2, gain.reshape(1, D_MODEL))


def _mixer(x2, b, s, norm1, w_in, table, gain_na, gain_dil, w_out, slopes):
    na_qkv, *cls = _qkv_proj(x2, b, s, norm1, w_in)
    a = _na_attention(na_qkv.reshape(b, s, 3 * D_NA), table, gain_na).reshape(b * s, D_NA)
    os_, lses = [], []
    for c, dilation in zip(cls, DILATIONS):
        o, lse = _dil_attention(c, dilation, slopes)
        os_.append(o)
        lses.append(lse)
    return _out_proj(a, os_, lses, s, gain_dil, w_out, x2)


def _moe(x2, norm2, wr_pad, wg, wu, wd):
    n = x2.shape[0]
    cap = CAPACITY_FACTOR * n // N_EXPERTS
    nt = n // MOE_TILE
    hext, aff = _route(x2, norm2, wr_pad)
    aff3 = aff.reshape(nt, MOE_TILE, N_EXPERTS).transpose(0, 2, 1)
    thr, need = _threshold(aff3, cap)
    dest3, start3, cnt3 = _positions(aff3, thr, need)
    starts = start3[:, :, 0].reshape(-1)
    cnts = cnt3[:, :, 0].reshape(-1)
    xe = _gather(starts, cnts, dest3, hext, cap)
    ye = _expert_ffn(xe, cap, wg, wu, wd)
    return _combine(starts, cnts, dest3.transpose(0, 2, 1), x2, ye, cap)


def _trunk(x, norm1, w_in, tables, gain_na, gain_dil, w_out, norm2, w_router, wg, wu, wd, final_norm):
    b, s, _ = x.shape
    slopes = _alibi_slopes(N_HEADS_DIL)
    x2 = x.reshape(b * s, D_MODEL)
    for l in range(DEPTH):
        x2 = _mixer(x2, b, s, norm1[l], w_in[l], tables[l], gain_na[l], gain_dil[l], w_out[l], slopes)
        x2 = _moe(x2, norm2[l], w_router[l], wg[l], wu[l], wd[l])
    return _final_norm(x2, final_norm).reshape(b, s, D_MODEL)


def kernel(x_prompt, x_sample, norm1, w_in, rpb, gain_na, gain_dil, w_out, norm2, w_router, w_gate, w_up, w_down, final_norm):
    w_in_b = w_in.astype(BF16)
    w_out_b = w_out.astype(BF16)
    wg, wu, wd = w_gate.astype(BF16), w_up.astype(BF16), w_down.astype(BF16)
    tables = jax.vmap(_na_bias_table)(rpb)
    pad = jnp.zeros((DEPTH, D_MODEL, LANES - GATE_PIECES * N_EXPERTS), F32)
    wr_pad = jnp.concatenate([w_router] * GATE_PIECES + [pad], axis=-1)
    args = (norm1, w_in_b, tables, gain_na, gain_dil, w_out_b, norm2, wr_pad, wg, wu, wd, final_norm)
    return (_trunk(x_prompt, *args), _trunk(x_sample, *args))
```

```python
import functools
import math

import jax
import jax.numpy as jnp
import numpy as np
from jax import lax
from jax.experimental import pallas as pl
from jax.experimental.pallas import tpu as pltpu

D_MODEL = 1024
DEPTH = 4
HEAD_DIM = 64
N_HEADS = D_MODEL // HEAD_DIM
N_HEADS_NA = N_HEADS // 4
N_HEADS_DIL = N_HEADS - N_HEADS_NA
D_NA = N_HEADS_NA * HEAD_DIM
D_DIL = N_HEADS_DIL * HEAD_DIM
GRID_W = 64
NA_ROWS = 8
NA_COLS = 16
DIL_PATTERNS = ((128, 1), (512, 4), (2048, 16))
N_EXPERTS = 16
CAPACITY_FACTOR = 2
RMS_EPS = 1e-6

LANES = 128
HEADS_PER_LANE_TILE = LANES // HEAD_DIM
DIL_GROUP = 256
N_DIL_GROUPS = D_DIL // DIL_GROUP
NEG_BIG = -1e30
VMEM_LIMIT = 56 * 1024 * 1024

BF16 = jnp.bfloat16
F32 = jnp.float32


def _alibi_slopes(n):
    def pow2(m):
        start = 2.0 ** (-8.0 / m)
        return [start ** (i + 1) for i in range(m)]
    if math.log2(n).is_integer():
        s = pow2(n)
    else:
        c = 2 ** int(math.floor(math.log2(n)))
        s = pow2(c) + pow2(2 * c)[0::2][: n - c]
    return np.asarray(s, dtype=np.float32)


def _params(*sem):
    return pltpu.CompilerParams(dimension_semantics=sem, vmem_limit_bytes=VMEM_LIMIT)


def _rms_scale(xf):
    return lax.rsqrt(jnp.mean(xf * xf, axis=-1, keepdims=True) + RMS_EPS)


N_DIL_SLABS = D_DIL // LANES
DILATIONS = tuple(d for _, d in DIL_PATTERNS)


def _qkv_kernel(x_ref, g_ref, w_ref, na_ref, *rest, tm):
    cls_refs, slab = rest[:-1], rest[-1]
    xf = x_ref[...]
    h = (xf * _rms_scale(xf) * g_ref[...]).astype(BF16)
    for c in range(3):
        for blk in range(D_MODEL // D_NA):
            lo_col = c * D_MODEL + blk * D_NA
            y = jnp.dot(h, w_ref[:, lo_col:lo_col + D_NA], preferred_element_type=F32)
            if c == 0:
                y = y * (HEAD_DIM ** -0.5)
            if blk == 0:
                na_ref[:, c * D_NA:(c + 1) * D_NA] = y.astype(BF16)
            else:
                for half in range(D_NA // LANES):
                    slab[(blk - 1) * (D_NA // LANES) + half] = y[:, half * LANES:(half + 1) * LANES]
        for ref, dil in zip(cls_refs, DILATIONS):
            for j in range(N_DIL_SLABS):
                for r in range(dil):
                    rows = slab[j] if dil == 1 else slab.at[j][pl.ds(r, tm // dil, stride=dil), :]
                    ref[0, c * N_DIL_SLABS + j, r] = rows.astype(BF16)


def _qkv_proj(x2, b, s, gain, w_bf16, tm=512):
    n = x2.shape[0]
    tpb = s // tm
    cls_shapes = [jax.ShapeDtypeStruct((b, 3 * N_DIL_SLABS, d, s // d, LANES), BF16) for d in DILATIONS]
    cls_specs = [pl.BlockSpec((1, 3 * N_DIL_SLABS, d, tm // d, LANES), lambda i: (i // tpb, 0, 0, i % tpb, 0))
                 for d in DILATIONS]
    return pl.pallas_call(
        functools.partial(_qkv_kernel, tm=tm),
        out_shape=[jax.ShapeDtypeStruct((n, 3 * D_NA), BF16)] + cls_shapes,
        grid=(n // tm,),
        in_specs=[
            pl.BlockSpec((tm, D_MODEL), lambda i: (i, 0)),
            pl.BlockSpec((1, D_MODEL), lambda i: (0, 0)),
            pl.BlockSpec((D_MODEL, 3 * D_MODEL), lambda i: (0, 0)),
        ],
        out_specs=[pl.BlockSpec((tm, 3 * D_NA), lambda i: (i, 0))] + cls_specs,
        scratch_shapes=[pltpu.VMEM((N_DIL_SLABS, tm, LANES), F32)],
        compiler_params=_params("parallel"),
        name="qkv_proj",
    )(x2, gain.reshape(1, D_MODEL), w_bf16)


def _na_bias_table(rpb):
    qc = np.arange(GRID_W)[:, None]
    kc = np.arange(GRID_W)[None, :]
    win0 = np.clip(qc - NA_COLS // 2, 0, GRID_W - NA_COLS)
    in_win = (kc >= win0) & (kc < win0 + NA_COLS)
    ci = np.clip(kc - qc + NA_COLS - 1, 0, 2 * NA_COLS - 2)
    n_ri, n_ci = 2 * NA_ROWS - 1, 2 * NA_COLS - 1
    pick = (ci.reshape(1, -1) == np.arange(n_ci)[:, None]).astype(np.float32)
    m = jnp.dot(rpb.astype(F32).reshape(N_HEADS_NA * n_ri, n_ci), pick, precision=lax.Precision.HIGHEST)
    m = jnp.where(in_win[None, None], m.reshape(N_HEADS_NA, n_ri, GRID_W, GRID_W), NEG_BIG)
    t = jnp.stack([m[:, v:v + NA_ROWS] for v in range(NA_ROWS)], axis=1)
    t = jnp.transpose(t, (0, 1, 3, 2, 4))
    t = t.reshape(N_HEADS_NA // HEADS_PER_LANE_TILE, HEADS_PER_LANE_TILE, NA_ROWS, GRID_W, NA_ROWS * GRID_W)
    return jnp.transpose(t, (0, 2, 1, 3, 4)).reshape(
        N_HEADS_NA // HEADS_PER_LANE_TILE, NA_ROWS, HEADS_PER_LANE_TILE * GRID_W, NA_ROWS * GRID_W)


def _head_rms(o, gain_tile, lo):
    o2 = o * o
    ms_lo = jnp.sum(jnp.where(lo, o2, 0.0), axis=-1, keepdims=True) * (1.0 / HEAD_DIM)
    ms_hi = jnp.sum(jnp.where(lo, 0.0, o2), axis=-1, keepdims=True) * (1.0 / HEAD_DIM)
    scale = jnp.where(lo, lax.rsqrt(ms_lo + RMS_EPS), lax.rsqrt(ms_hi + RMS_EPS))
    return o * scale * gain_tile


def _na_kernel(q_ref, k_ref, v_ref, t_ref, g_ref, o_ref, *, rows):
    nk = NA_ROWS * GRID_W
    lo = lax.broadcasted_iota(jnp.int32, (1, LANES), 1) < HEAD_DIM

    n_slabs = D_NA // LANES
    per_trip = 2

    def row_body(rr, carry):
        where, scores, values = [], [], []
        for u in range(per_trip):
            r = rr * per_trip + u
            kr0 = jnp.clip(r - NA_ROWS // 2, 0, rows - NA_ROWS)
            q0 = pl.multiple_of(r * GRID_W, GRID_W)
            k0 = pl.multiple_of(kr0 * GRID_W, GRID_W)
            where.append((q0, kr0 - r + NA_ROWS - 1))
            for p in range(n_slabs):
                lanes = slice(p * LANES, (p + 1) * LANES)
                q2 = q_ref[0, pl.ds(q0, GRID_W), lanes]
                zero = jnp.zeros_like(q2)
                q_both = jnp.concatenate([jnp.where(lo, q2, zero), jnp.where(lo, zero, q2)], axis=0)
                scores.append(lax.dot_general(q_both, k_ref[0, pl.ds(k0, nk), lanes], (((1,), (1,)), ((), ())),
                                              preferred_element_type=F32))
                values.append(v_ref[0, pl.ds(k0, nk), lanes])
        probs, sums = [], []
        for u in range(per_trip):
            for p in range(n_slabs):
                s = scores[u * n_slabs + p] + t_ref[p, where[u][1]]
                e = jnp.exp(s - jnp.max(s, axis=-1, keepdims=True))
                sums.append(jnp.sum(e, axis=-1, keepdims=True))
                probs.append(e.astype(BF16))
        for u in range(per_trip):
            for p in range(n_slabs):
                lanes = slice(p * LANES, (p + 1) * LANES)
                o = jnp.dot(probs[u * n_slabs + p], values[u * n_slabs + p],
                            preferred_element_type=F32) / sums[u * n_slabs + p]
                o = jnp.where(lo, o[:GRID_W], o[GRID_W:])
                o_ref[0, pl.ds(where[u][0], GRID_W), lanes] = _head_rms(o, g_ref[:, lanes], lo).astype(BF16)
        return carry

    lax.fori_loop(0, rows // per_trip, row_body, 0)


def _na_attention(qkv, table, gain_na):
    b, s, _ = qkv.shape
    rows = s // GRID_W
    return pl.pallas_call(
        functools.partial(_na_kernel, rows=rows),
        out_shape=jax.ShapeDtypeStruct((b, s, D_NA), BF16),
        grid=(b,),
        in_specs=[
            pl.BlockSpec((1, s, D_NA), lambda i: (i, 0, 0)),
            pl.BlockSpec((1, s, D_NA), lambda i: (i, 0, 1)),
            pl.BlockSpec((1, s, D_NA), lambda i: (i, 0, 2)),
            pl.BlockSpec(table.shape, lambda i: (0, 0, 0, 0)),
            pl.BlockSpec((1, D_NA), lambda i: (0, 0)),
        ],
        out_specs=pl.BlockSpec((1, s, D_NA), lambda i: (i, 0, 0)),
        compiler_params=_params("parallel"),
        name="na_attention",
    )(qkv, qkv, qkv, table, gain_na.reshape(1, D_NA))


DIL_HALF = 64
SLABS_PER_GROUP = DIL_GROUP // LANES
HEADS_PER_GROUP = DIL_GROUP // HEAD_DIM


def _dil_kernel(q_ref, k_ref, v_ref, o_ref, lse_ref, bias, *, length, dilation, slopes, tq):
    kw = min(length, tq + 2 * DIL_HALF)
    n_off = 1 if length <= tq else 3
    lo = lax.broadcasted_iota(jnp.int32, (1, LANES), 1) < HEAD_DIM
    lane = lax.broadcasted_iota(jnp.int32, (1, LANES), 1)
    grp = pl.program_id(1)
    qi = lax.broadcasted_iota(jnp.int32, (tq, kw), 0)
    ki = lax.broadcasted_iota(jnp.int32, (tq, kw), 1)

    for v in range(n_off):
        dist = jnp.abs(qi + v * DIL_HALF - ki)
        reach = dist <= DIL_HALF
        distf = dist.astype(F32) * float(dilation)
        for hl in range(HEADS_PER_GROUP):
            slope = jnp.float32(slopes[hl])
            for g in range(1, N_DIL_GROUPS):
                slope = jnp.where(grp == g, jnp.float32(slopes[g * HEADS_PER_GROUP + hl]), slope)
            p, hh = divmod(hl, HEADS_PER_LANE_TILE)
            bias[v, p, hh * tq:(hh + 1) * tq, :] = jnp.where(reach, -slope * distf, NEG_BIG)

    n_blk = dilation * (length // tq)
    per_trip = 4 if n_blk % 4 == 0 else 1

    def blk_body(tt, carry):
        where, scores, values = [], [], []
        for u in range(per_trip):
            t = tt * per_trip + u
            r = t // (length // tq)
            q0 = pl.multiple_of((t % (length // tq)) * tq, tq)
            k0 = pl.multiple_of(jnp.clip(q0 - DIL_HALF, 0, length - kw), DIL_HALF)
            where.append((r, q0, (q0 - k0) // DIL_HALF))
            for p in range(SLABS_PER_GROUP):
                q2 = q_ref[0, p, r, pl.ds(q0, tq), :]
                k2 = k_ref[0, p, r, pl.ds(k0, kw), :]
                zero = jnp.zeros_like(q2)
                q_both = jnp.concatenate([jnp.where(lo, q2, zero), jnp.where(lo, zero, q2)], axis=0)
                scores.append(lax.dot_general(q_both, k2, (((1,), (1,)), ((), ())), preferred_element_type=F32))
                values.append(v_ref[0, p, r, pl.ds(k0, kw), :])
        probs, stats = [], []
        for u in range(per_trip):
            for p in range(SLABS_PER_GROUP):
                s = scores[u * SLABS_PER_GROUP + p] + bias[where[u][2], p]
                m = jnp.max(s, axis=-1, keepdims=True)
                e = jnp.exp(s - m)
                stats.append((m, jnp.sum(e, axis=-1, keepdims=True)))
                probs.append(e.astype(BF16))
        for u in range(per_trip):
            r, q0, _ = where[u]
            lse_tile = jnp.zeros((tq, LANES), F32)
            for p in range(SLABS_PER_GROUP):
                m, l = stats[u * SLABS_PER_GROUP + p]
                o = jnp.dot(probs[u * SLABS_PER_GROUP + p], values[u * SLABS_PER_GROUP + p],
                            preferred_element_type=F32) / l
                o_ref[0, p, r, pl.ds(q0, tq), :] = jnp.where(lo, o[:tq], o[tq:])
                lse = m + jnp.log(l)
                lse_tile = jnp.where(lane == 2 * p, lse[:tq], jnp.where(lane == 2 * p + 1, lse[tq:], lse_tile))
            lse_ref[0, 0, r, pl.ds(q0, tq), :] = lse_tile
        return carry

    lax.fori_loop(0, n_blk // per_trip, blk_body, 0)


def _dil_attention(cls, dilation, slopes, tq=128):
    b, _, _, length, _ = cls.shape

    def spec(part):
        return pl.BlockSpec((1, SLABS_PER_GROUP, dilation, length, LANES),
                            lambda i, g: (i, part * N_DIL_GROUPS + g, 0, 0, 0))

    n_off = 1 if length <= tq else 3
    return pl.pallas_call(
        functools.partial(_dil_kernel, length=length, dilation=dilation,
                          slopes=tuple(float(x) for x in slopes), tq=tq),
        out_shape=(jax.ShapeDtypeStruct((b, N_DIL_SLABS, dilation, length, LANES), F32),
                   jax.ShapeDtypeStruct((b, N_DIL_GROUPS, dilation, length, LANES), F32)),
        grid=(b, N_DIL_GROUPS),
        in_specs=[spec(0), spec(1), spec(2)],
        out_specs=(pl.BlockSpec((1, SLABS_PER_GROUP, dilation, length, LANES), lambda i, g: (i, g, 0, 0, 0)),
                   pl.BlockSpec((1, 1, dilation, length, LANES), lambda i, g: (i, g, 0, 0, 0))),
        scratch_shapes=[pltpu.VMEM((n_off, SLABS_PER_GROUP, HEADS_PER_LANE_TILE * tq,
                                    min(length, tq + 2 * DIL_HALF)), F32)],
        compiler_params=_params("parallel", "parallel"),
        name=f"dil_attention_d{dilation}",
    )(cls, cls, cls)


def _out_kernel(a_ref, *rest, tm):
    n_pat = len(DILATIONS)
    o_refs, l_refs = rest[:n_pat], rest[n_pat:2 * n_pat]
    g_ref, w_ref, e_ref, x_ref, y_ref, nat_o, nat_l, mixed = rest[2 * n_pat:]
    lo = lax.broadcasted_iota(jnp.int32, (1, LANES), 1) < HEAD_DIM

    for k, dil in enumerate(DILATIONS):
        if dil == 1:
            continue
        for r in range(dil):
            for j in range(N_DIL_SLABS):
                nat_o.at[k, j][pl.ds(r, tm // dil, stride=dil), :] = o_refs[k][0, j, r]
            for g in range(N_DIL_GROUPS):
                nat_l.at[k, g][pl.ds(r, tm // dil, stride=dil), :] = l_refs[k][0, g, r]

    def token_rows(refs, nat, k, j):
        return refs[k][0, j, 0] if DILATIONS[k] == 1 else nat[k, j]

    mixed[:, 0:D_NA] = a_ref[...]
    for g in range(N_DIL_GROUPS):
        ls = [token_rows(l_refs, nat_l, k, g) for k in range(n_pat)]
        m = functools.reduce(jnp.maximum, ls)
        es = [jnp.exp(l - m) for l in ls]
        inv = 1.0 / functools.reduce(lambda u, v: u + v, es)
        ws = [e * inv for e in es]
        spread = []
        for w in ws:
            hi = w.astype(BF16)
            lo_part = (w - hi.astype(F32)).astype(BF16)
            spread.append(jnp.dot(hi, e_ref[...], preferred_element_type=F32)
                          + jnp.dot(lo_part, e_ref[...], preferred_element_type=F32))
        for p in range(SLABS_PER_GROUP):
            j = g * SLABS_PER_GROUP + p
            d = jnp.zeros((tm, LANES), F32)
            for k in range(n_pat):
                d = d + spread[k][:, p * LANES:(p + 1) * LANES] * token_rows(o_refs, nat_o, k, j)
            d = _head_rms(d, g_ref[:, j * LANES:(j + 1) * LANES], lo)
            mixed[:, D_NA + j * LANES:D_NA + (j + 1) * LANES] = d.astype(BF16)
    y_ref[...] = x_ref[...] + jnp.dot(mixed[...], w_ref[...], preferred_element_type=F32)


def _out_proj(a, os_, lses, s, gain_dil, w_bf16, x2, tm=256):
    n = x2.shape[0]
    tpb = s // tm
    row = lambda i: (i, 0)
    fixed = lambda i: (0, 0)
    cls = lambda i: (i // tpb, 0, 0, i % tpb, 0)
    head_cols = jnp.asarray(np.arange(LANES)[:, None] == np.arange(DIL_GROUP)[None, :] // HEAD_DIM, BF16)
    return pl.pallas_call(
        functools.partial(_out_kernel, tm=tm),
        out_shape=jax.ShapeDtypeStruct((n, D_MODEL), F32),
        grid=(n // tm,),
        in_specs=[pl.BlockSpec((tm, D_NA), row)]
        + [pl.BlockSpec((1, N_DIL_SLABS, d, tm // d, LANES), cls) for d in DILATIONS]
        + [pl.BlockSpec((1, N_DIL_GROUPS, d, tm // d, LANES), cls) for d in DILATIONS]
        + [pl.BlockSpec((1, D_DIL), fixed), pl.BlockSpec((D_MODEL, D_MODEL), fixed),
           pl.BlockSpec((LANES, DIL_GROUP), fixed), pl.BlockSpec((tm, D_MODEL), row)],
        out_specs=pl.BlockSpec((tm, D_MODEL), row),
        scratch_shapes=[pltpu.VMEM((len(DILATIONS), N_DIL_SLABS, tm, LANES), F32),
                        pltpu.VMEM((len(DILATIONS), N_DIL_GROUPS, tm, LANES), F32),
                        pltpu.VMEM((tm, D_MODEL), BF16)],
        compiler_params=_params("parallel"),
        name="out_proj",
    )(a, *os_, *lses, gain_dil.reshape(1, D_DIL), w_bf16, head_cols, x2)


MOE_TILE = 256
MOE_CHUNK = 64
HEXT = D_MODEL + LANES
GATE_PIECES = 3


def _route_kernel(x_ref, g_ref, wr_ref, hext_ref, aff_ref):
    xf = x_ref[...]
    h = xf * _rms_scale(xf) * g_ref[...]
    hext_ref[:, :D_MODEL] = h.astype(BF16)
    logits = jnp.dot(h, wr_ref[...], preferred_element_type=F32, precision=lax.Precision.HIGHEST)
    lane = lax.broadcasted_iota(jnp.int32, (1, LANES), 1)
    first = lane < N_EXPERTS
    used = lane < GATE_PIECES * N_EXPERTS
    m = jnp.max(jnp.where(first, logits, NEG_BIG), axis=-1, keepdims=True)
    e = jnp.exp(jnp.where(used, logits - m, 0.0))
    aff = e / jnp.sum(jnp.where(first, e, 0.0), axis=-1, keepdims=True)
    aff_ref[...] = aff[:, :N_EXPERTS]
    hi = aff.astype(BF16)
    r1 = aff - hi.astype(F32)
    mid = r1.astype(BF16)
    lo = (r1 - mid.astype(F32)).astype(BF16)
    pieces = jnp.where(first, hi, jnp.where(lane < 2 * N_EXPERTS, mid, lo))
    hext_ref[:, D_MODEL:] = jnp.where(used, pieces, jnp.zeros_like(pieces))


def _route(x2, gain, wr_pad, tm=512):
    n = x2.shape[0]
    return pl.pallas_call(
        _route_kernel,
        out_shape=(jax.ShapeDtypeStruct((n, HEXT), BF16),
                   jax.ShapeDtypeStruct((n, N_EXPERTS), F32)),
        grid=(n // tm,),
        in_specs=[pl.BlockSpec((tm, D_MODEL), lambda i: (i, 0)),
                  pl.BlockSpec((1, D_MODEL), lambda i: (0, 0)),
                  pl.BlockSpec((D_MODEL, LANES), lambda i: (0, 0))],
        out_specs=(pl.BlockSpec((tm, HEXT), lambda i: (i, 0)),
                   pl.BlockSpec((tm, N_EXPERTS), lambda i: (i, 0))),
        compiler_params=_params("parallel"),
        name="route",
    )(x2, gain.reshape(1, D_MODEL), wr_pad)


def _thr_kernel(aff_ref, thr_ref, need_ref, *, cap):
    bits = lax.bitcast_convert_type(aff_ref[...], jnp.int32)

    def count(mask):
        per_lane = jnp.sum(mask.astype(jnp.int32), axis=0)
        return jnp.sum(per_lane, axis=-1, keepdims=True)

    def body(i, prefix):
        cand = prefix | jnp.left_shift(jnp.int32(1), 30 - i)
        return jnp.where(count(bits >= cand[None]) >= cap, cand, prefix)

    thr = lax.fori_loop(0, 31, body, jnp.zeros((N_EXPERTS, 1), jnp.int32))
    thr_ref[...] = jnp.broadcast_to(thr, (N_EXPERTS, LANES))
    need_ref[...] = jnp.broadcast_to(cap - count(bits > thr[None]), (N_EXPERTS, LANES))


def _threshold(aff3, cap):
    out = jax.ShapeDtypeStruct((N_EXPERTS, LANES), jnp.int32)
    return pl.pallas_call(
        functools.partial(_thr_kernel, cap=cap),
        out_shape=(out, out),
        compiler_params=_params(),
        name="route_threshold",
    )(aff3)


def _pos_kernel(aff_ref, thr_ref, need_ref, tri_ref, dest_ref, start_ref, cnt_ref, run_eq, run_sel, *, tc):
    @pl.when(pl.program_id(0) == 0)
    def _():
        run_eq[...] = jnp.zeros_like(run_eq)
        run_sel[...] = jnp.zeros_like(run_sel)

    bits = lax.bitcast_convert_type(aff_ref[...], jnp.int32)
    thr = thr_ref[:, :1]
    need = need_ref[:, :1].astype(F32)
    gt = bits > thr[None]
    eq = bits == thr[None]

    def prefix(mask):
        f = mask.astype(F32)
        incl = jnp.dot(f.astype(BF16).reshape(tc * N_EXPERTS, MOE_TILE), tri_ref[...],
                       preferred_element_type=F32)
        return f, incl.reshape(tc, N_EXPERTS, MOE_TILE)

    eqf, eq_incl = prefix(eq)
    base = run_eq[:, :1]
    sels = []
    for j in range(tc):
        before = base + eq_incl[j] - eqf[j]
        sels.append(gt[j] | (eq[j] & (before < need)))
        base = base + eq_incl[j, :, MOE_TILE - 1:]
    run_eq[...] = jnp.broadcast_to(base, run_eq.shape)
    sel = jnp.stack(sels)
    _, incl = prefix(sel)
    dest_ref[...] = jnp.where(sel, incl - 1.0, -1.0).astype(jnp.int32)
    base = run_sel[:, :1]
    for j in range(tc):
        cnt = incl[j, :, MOE_TILE - 1:]
        start_ref[j] = jnp.broadcast_to(base, (N_EXPERTS, LANES)).astype(jnp.int32)
        cnt_ref[j] = jnp.broadcast_to(cnt, (N_EXPERTS, LANES)).astype(jnp.int32)
        base = base + cnt
    run_sel[...] = jnp.broadcast_to(base, run_sel.shape)


def _positions(aff3, thr, need, tc=8):
    nt = aff3.shape[0]
    tri = jnp.asarray(np.triu(np.ones((MOE_TILE, MOE_TILE), np.float32)), BF16)
    blk = lambda i: (i, 0, 0)
    fixed = lambda i: (0, 0)
    small = jax.ShapeDtypeStruct((nt, N_EXPERTS, LANES), jnp.int32)
    return pl.pallas_call(
        functools.partial(_pos_kernel, tc=tc),
        out_shape=(jax.ShapeDtypeStruct((nt, N_EXPERTS, MOE_TILE), jnp.int32), small, small),
        grid=(nt // tc,),
        in_specs=[pl.BlockSpec((tc, N_EXPERTS, MOE_TILE), blk),
                  pl.BlockSpec((N_EXPERTS, LANES), fixed), pl.BlockSpec((N_EXPERTS, LANES), fixed),
                  pl.BlockSpec((MOE_TILE, MOE_TILE), fixed)],
        out_specs=(pl.BlockSpec((tc, N_EXPERTS, MOE_TILE), blk),
                   pl.BlockSpec((tc, N_EXPERTS, LANES), blk), pl.BlockSpec((tc, N_EXPERTS, LANES), blk)),
        scratch_shapes=[pltpu.VMEM((N_EXPERTS, LANES), F32), pltpu.VMEM((N_EXPERTS, LANES), F32)],
        compiler_params=_params("arbitrary"),
        name="route_positions",
    )(aff3, thr, need, tri)


ROW_ALIGN = 16
MOE_WINDOW = MOE_CHUNK + ROW_ALIGN
MOE_STAGE_ROWS = N_EXPERTS * MOE_WINDOW


def _rounds(cnt_sm, tile):
    most = cnt_sm[tile * N_EXPERTS]
    for e in range(1, N_EXPERTS):
        most = jnp.maximum(most, cnt_sm[tile * N_EXPERTS + e])
    return (most + MOE_CHUNK - 1) // MOE_CHUNK


def _round_window(start_sm, cnt_sm, tile, e, r):
    base = jnp.minimum(r * MOE_CHUNK, cnt_sm[tile * N_EXPERTS + e])
    return base, start_sm[tile * N_EXPERTS + e] + base


def _align_down(v):
    return v & -ROW_ALIGN


def _align_rest(v):
    return v & (ROW_ALIGN - 1)


def _gather_kernel(start_sm, cnt_sm, dest_ref, hext_ref, xe_hbm, stage, sems, pad_sem, cur, busy, prev_row):
    i = pl.program_id(0)
    kio = lax.broadcasted_iota(jnp.int32, (MOE_WINDOW, MOE_TILE), 0)
    rid = lax.broadcasted_iota(jnp.int32, (ROW_ALIGN, HEXT), 0)

    def copy(e, s, row):
        return pltpu.make_async_copy(stage.at[s, e],
                                     xe_hbm.at[e, pl.ds(pl.multiple_of(row, ROW_ALIGN), MOE_WINDOW)], sems.at[s, e])

    @pl.when(i == 0)
    def _():
        for e in range(N_EXPERTS):
            cur[e] = 0
            busy[e] = 0
            prev_row[e] = 0
        stage[...] = jnp.zeros_like(stage)
        pads = [pltpu.make_async_copy(stage.at[0, e], xe_hbm.at[e, pl.ds(xe_hbm.shape[1] - MOE_WINDOW, MOE_WINDOW)],
                                      pad_sem) for e in range(N_EXPERTS)]
        for c in pads:
            c.start()
        for c in pads:
            c.wait()

    def onehot_rows(e, r):
        base, first = _round_window(start_sm, cnt_sm, i, e, r)
        local = dest_ref[0, e:e + 1, :] - base
        ok = (local >= 0) & (local < MOE_CHUNK)
        return (jnp.where(ok, local + _align_rest(first), -1) == kio).astype(F32).astype(BF16), first

    def place(e, first, rows):
        old_slot = cur[e]
        new_slot = 1 - old_slot
        aligned = _align_down(first)
        off = pl.multiple_of(aligned - prev_row[e], ROW_ALIGN)
        old = stage[old_slot, e, pl.ds(off, ROW_ALIGN), :]
        stage[new_slot, e, 0:ROW_ALIGN, :] = jnp.where(rid < _align_rest(first), old, rows[0:ROW_ALIGN])
        stage[new_slot, e, ROW_ALIGN:MOE_WINDOW, :] = rows[ROW_ALIGN:MOE_WINDOW]

        @pl.when(busy[e] > 0)
        def _():
            copy(e, old_slot, 0).wait()

        copy(e, new_slot, aligned).start()
        prev_row[e] = aligned
        cur[e] = new_slot
        busy[e] = 1

    picks = [onehot_rows(e, 0) for e in range(N_EXPERTS)]
    res = jnp.dot(jnp.concatenate([p[0] for p in picks], axis=0), hext_ref[...],
                  preferred_element_type=F32).astype(BF16)
    for e in range(N_EXPERTS):
        place(e, picks[e][1], res[e * MOE_WINDOW:(e + 1) * MOE_WINDOW])

    def extra_round(r, carry):
        for e in range(N_EXPERTS):
            @pl.when(cnt_sm[i * N_EXPERTS + e] > r * MOE_CHUNK)
            def _():
                onehot, first = onehot_rows(e, r)
                place(e, first, jnp.dot(onehot, hext_ref[...], preferred_element_type=F32).astype(BF16))
        return carry

    lax.fori_loop(1, _rounds(cnt_sm, i), extra_round, 0)

    @pl.when(i == pl.num_programs(0) - 1)
    def _():
        for e in range(N_EXPERTS):
            copy(e, cur[e], 0).wait()


def _gather(starts, cnts, dest3, hext, cap):
    nt = dest3.shape[0]
    return pl.pallas_call(
        _gather_kernel,
        out_shape=jax.ShapeDtypeStruct((N_EXPERTS, cap + MOE_WINDOW, HEXT), BF16),
        grid_spec=pltpu.PrefetchScalarGridSpec(
            num_scalar_prefetch=2,
            grid=(nt,),
            in_specs=[pl.BlockSpec((1, N_EXPERTS, MOE_TILE), lambda i, *_: (i, 0, 0)),
                      pl.BlockSpec((MOE_TILE, HEXT), lambda i, *_: (i, 0))],
            out_specs=pl.BlockSpec(memory_space=pl.ANY),
            scratch_shapes=[pltpu.VMEM((2, N_EXPERTS, MOE_WINDOW, HEXT), BF16),
                            pltpu.SemaphoreType.DMA((2, N_EXPERTS)), pltpu.SemaphoreType.DMA(()),
                            pltpu.SMEM((N_EXPERTS,), jnp.int32), pltpu.SMEM((N_EXPERTS,), jnp.int32),
                            pltpu.SMEM((N_EXPERTS,), jnp.int32)],
        ),
        compiler_params=_params("arbitrary"),
        name="route_gather",
    )(starts, cnts, dest3, hext)


def _ffn_kernel(xe_ref, wg_ref, wu_ref, wd_ref, ye_ref):
    xe = xe_ref[:, :D_MODEL]
    pieces = xe_ref[:, D_MODEL:].astype(F32)
    lane = lax.broadcasted_iota(jnp.int32, (1, LANES), 1)
    mine = (lane % N_EXPERTS == pl.program_id(0)) & (lane < GATE_PIECES * N_EXPERTS)
    g = jnp.sum(jnp.where(mine, pieces, 0.0), axis=-1, keepdims=True)
    gate = jnp.dot(xe, wg_ref[...], preferred_element_type=F32)
    up = jnp.dot(xe, wu_ref[...], preferred_element_type=F32)
    hid = (gate * jax.nn.sigmoid(gate) * up).astype(BF16)
    ye_ref[...] = (jnp.dot(hid, wd_ref[...], preferred_element_type=F32) * g).astype(BF16)


def _expert_ffn(xe, cap, wg, wu, wd, tm=512):
    tm = min(tm, cap)
    d = D_MODEL
    tile = lambda i, j: (i, j, 0)
    wspec = pl.BlockSpec((None, d, d), lambda i, j: (i, 0, 0))
    return pl.pallas_call(
        _ffn_kernel,
        out_shape=jax.ShapeDtypeStruct((N_EXPERTS, cap, d), BF16),
        grid=(N_EXPERTS, cap // tm),
        in_specs=[pl.BlockSpec((None, tm, HEXT), tile), wspec, wspec, wspec],
        out_specs=pl.BlockSpec((None, tm, d), tile),
        compiler_params=_params("parallel", "arbitrary"),
        name="expert_ffn",
    )(xe, wg, wu, wd)


def _combine_kernel(start_sm, cnt_sm, dest_ref, x_ref, ye_hbm, out_ref, stage, sems, spill, spill_sems, *, cap):
    i = pl.program_id(0)
    slot = i % 2
    lane = lax.broadcasted_iota(jnp.int32, (1, LANES), 1)
    lane_e = lax.broadcasted_iota(jnp.int32, (1, N_EXPERTS), 1)

    def window(tile, e, r):
        base, first = _round_window(start_sm, cnt_sm, tile, e, r)
        return base, first, jnp.minimum(_align_down(first), cap - MOE_WINDOW)

    def copy(e, s, row):
        return pltpu.make_async_copy(ye_hbm.at[e, pl.ds(pl.multiple_of(row, ROW_ALIGN), MOE_WINDOW)],
                                     stage.at[s, pl.ds(e * MOE_WINDOW, MOE_WINDOW)], sems.at[s])

    def fetch(tile, s, r):
        for e in range(N_EXPERTS):
            copy(e, s, window(tile, e, r)[2]).start()

    def wait_all(s):
        for e in range(N_EXPERTS):
            copy(e, s, 0).wait()

    def expand(r):
        lo = jnp.zeros((1, N_EXPERTS), jnp.int32)
        shift = jnp.zeros((1, N_EXPERTS), jnp.int32)
        for e in range(N_EXPERTS):
            base, first, row = window(i, e, r)
            lo = jnp.where(lane_e == e, base, lo)
            shift = jnp.where(lane_e == e, first - row + e * MOE_WINDOW - base, shift)
        rank = dest_ref[0]
        col = jnp.where((rank >= lo) & (rank < lo + MOE_CHUNK), rank + shift, -1)
        tiles = []
        for j in range(MOE_STAGE_ROWS // LANES):
            hit = None
            for e in range(j * LANES // MOE_WINDOW, min(N_EXPERTS - 1, ((j + 1) * LANES - 1) // MOE_WINDOW) + 1):
                m = col[:, e:e + 1] == lane + j * LANES
                hit = m if hit is None else hit | m
            tiles.append(hit)
        onehot = jnp.concatenate(tiles, axis=1).astype(F32).astype(BF16)
        return jnp.dot(onehot, stage[slot], preferred_element_type=F32)

    @pl.when(i == 0)
    def _():
        fetch(0, 0, 0)

    @pl.when(i + 1 < pl.num_programs(0))
    def _():
        fetch(i + 1, 1 - slot, 0)

    wait_all(slot)
    out_ref[...] = x_ref[...] + expand(0)

    @pl.when(i == 0)
    def _():
        spill[...] = jnp.zeros_like(spill)

    def spill_copy(e, row):
        return pltpu.make_async_copy(ye_hbm.at[e, pl.ds(pl.multiple_of(row, ROW_ALIGN), MOE_WINDOW)],
                                     spill.at[e, pl.ds(0, MOE_WINDOW)], spill_sems.at[e])

    def extra_round(r, carry):
        for e in range(N_EXPERTS):
            @pl.when(cnt_sm[i * N_EXPERTS + e] > r * MOE_CHUNK)
            def _():
                spill_copy(e, window(i, e, r)[2]).start()
        for e in range(N_EXPERTS):
            @pl.when(cnt_sm[i * N_EXPERTS + e] > r * MOE_CHUNK)
            def _():
                base, first, row = window(i, e, r)
                rank = dest_ref[0, :, e:e + 1]
                col = jnp.where((rank >= base) & (rank < base + MOE_CHUNK), rank - base + first - row, -1)
                spill_copy(e, 0).wait()
                out_ref[...] += jnp.dot((col == lane).astype(F32).astype(BF16), spill[e],
                                        preferred_element_type=F32)
        return carry

    lax.fori_loop(1, _rounds(cnt_sm, i), extra_round, 0)


def _combine(starts, cnts, dest_t, x2, ye, cap):
    nt = dest_t.shape[0]
    return pl.pallas_call(
        functools.partial(_combine_kernel, cap=cap),
        out_shape=jax.ShapeDtypeStruct(x2.shape, F32),
        grid_spec=pltpu.PrefetchScalarGridSpec(
            num_scalar_prefetch=2,
            grid=(nt,),
            in_specs=[pl.BlockSpec((1, MOE_TILE, N_EXPERTS), lambda i, *_: (i, 0, 0)),
                      pl.BlockSpec((MOE_TILE, D_MODEL), lambda i, *_: (i, 0)),
                      pl.BlockSpec(memory_space=pl.ANY)],
            out_specs=pl.BlockSpec((MOE_TILE, D_MODEL), lambda i, *_: (i, 0)),
            scratch_shapes=[pltpu.VMEM((2, MOE_STAGE_ROWS, D_MODEL), BF16), pltpu.SemaphoreType.DMA((2,)),
                            pltpu.VMEM((N_EXPERTS, LANES, D_MODEL), BF16), pltpu.SemaphoreType.DMA((N_EXPERTS,))],
        ),
        compiler_params=_params("arbitrary"),
        name="route_combine",
    )(starts, cnts, dest_t, x2, ye)


def _final_kernel(x_ref, g_ref, o_ref):
    xf = x_ref[...]
    o_ref[...] = xf * _rms_scale(xf) * g_ref[...]


def _final_norm(x2, gain, tm=1024):
    n = x2.shape[0]
    return pl.pallas_call(
        _final_kernel,
        out_shape=jax.ShapeDtypeStruct((n, D_MODEL), F32),
        grid=(n // tm,),
        in_specs=[pl.BlockSpec((tm, D_MODEL), lambda i: (i, 0)),
                  pl.BlockSpec((1, D_MODEL), lambda i: (0, 0))],
        out_specs=pl.BlockSpec((tm, D_MODEL), lambda i: (i, 0)),
        compiler_params=_params("parallel"),
        name="final_norm",
    )(x2, gain.reshape(1, D_MODEL))


def _mixer(x2, b, s, norm1, w_in, table, gain_na, gain_dil, w_out, slopes):
    na_qkv, *cls = _qkv_proj(x2, b, s, norm1, w_in)
    a = _na_attention(na_qkv.reshape(b, s, 3 * D_NA), table, gain_na).reshape(b * s, D_NA)
    os_, lses = [], []
    for c, dilation in zip(cls, DILATIONS):
        o, lse = _dil_attention(c, dilation, slopes)
        os_.append(o)
        lses.append(lse)
    return _out_proj(a, os_, lses, s, gain_dil, w_out, x2)


def _moe(x2, norm2, wr_pad, wg, wu, wd):
    n = x2.shape[0]
    cap = CAPACITY_FACTOR * n // N_EXPERTS
    nt = n // MOE_TILE
    hext, aff = _route(x2, norm2, wr_pad)
    aff3 = aff.reshape(nt, MOE_TILE, N_EXPERTS).transpose(0, 2, 1)
    thr, need = _threshold(aff3, cap)
    dest3, start3, cnt3 = _positions(aff3, thr, need)
    starts = start3[:, :, 0].reshape(-1)
    cnts = cnt3[:, :, 0].reshape(-1)
    xe = _gather(starts, cnts, dest3, hext, cap)
    ye = _expert_ffn(xe, cap, wg, wu, wd)
    return _combine(starts, cnts, dest3.transpose(0, 2, 1), x2, ye, cap)


def _trunk(x, norm1, w_in, tables, gain_na, gain_dil, w_out, norm2, w_router, wg, wu, wd, final_norm):
    b, s, _ = x.shape
    slopes = _alibi_slopes(N_HEADS_DIL)
    x2 = x.reshape(b * s, D_MODEL)
    for l in range(DEPTH):
        x2 = _mixer(x2, b, s, norm1[l], w_in[l], tables[l], gain_na[l], gain_dil[l], w_out[l], slopes)
        x2 = _moe(x2, norm2[l], w_router[l], wg[l], wu[l], wd[l])
    return _final_norm(x2, final_norm).reshape(b, s, D_MODEL)


def kernel(x_prompt, x_sample, norm1, w_in, rpb, gain_na, gain_dil, w_out, norm2, w_router, w_gate, w_up, w_down, final_norm):
    w_in_b = w_in.astype(BF16)
    w_out_b = w_out.astype(BF16)
    wg, wu, wd = w_gate.astype(BF16), w_up.astype(BF16), w_down.astype(BF16)
    tables = jax.vmap(_na_bias_table)(rpb)
    pad = jnp.zeros((DEPTH, D_MODEL, LANES - GATE_PIECES * N_EXPERTS), F32)
    wr_pad = jnp.concatenate([w_router] * GATE_PIECES + [pad], axis=-1)
    args = (norm1, w_in_b, tables, gain_na, gain_dil, w_out_b, norm2, wr_pad, wg, wu, wd, final_norm)
    return (_trunk(x_prompt, *args), _trunk(x_sample, *args))
```

```python
import functools
import math

import jax
import jax.numpy as jnp
import numpy as np
from jax import lax
from jax.experimental import pallas as pl
from jax.experimental.pallas import tpu as pltpu

D_MODEL = 1024
DEPTH = 4
HEAD_DIM = 64
N_HEADS = D_MODEL // HEAD_DIM
N_HEADS_NA = N_HEADS // 4
N_HEADS_DIL = N_HEADS - N_HEADS_NA
D_NA = N_HEADS_NA * HEAD_DIM
D_DIL = N_HEADS_DIL * HEAD_DIM
GRID_W = 64
NA_ROWS = 8
NA_COLS = 16
DIL_PATTERNS = ((128, 1), (512, 4), (2048, 16))
N_EXPERTS = 16
CAPACITY_FACTOR = 2
RMS_EPS = 1e-6

LANES = 128
HEADS_PER_LANE_TILE = LANES // HEAD_DIM
DIL_GROUP = 256
N_DIL_GROUPS = D_DIL // DIL_GROUP
NEG_BIG = -1e30
VMEM_LIMIT = 56 * 1024 * 1024

BF16 = jnp.bfloat16
F32 = jnp.float32


def _alibi_slopes(n):
    def pow2(m):
        start = 2.0 ** (-8.0 / m)
        return [start ** (i + 1) for i in range(m)]
    if math.log2(n).is_integer():
        s = pow2(n)
    else:
        c = 2 ** int(math.floor(math.log2(n)))
        s = pow2(c) + pow2(2 * c)[0::2][: n - c]
    return np.asarray(s, dtype=np.float32)


def _params(*sem):
    return pltpu.CompilerParams(dimension_semantics=sem, vmem_limit_bytes=VMEM_LIMIT)


def _rms_scale(xf):
    return lax.rsqrt(jnp.mean(xf * xf, axis=-1, keepdims=True) + RMS_EPS)


N_DIL_SLABS = D_DIL // LANES
DILATIONS = tuple(d for _, d in DIL_PATTERNS)


def _qkv_kernel(x_ref, g_ref, w_ref, na_ref, *rest, tm):
    cls_refs, slab = rest[:-1], rest[-1]
    xf = x_ref[...]
    h = (xf * _rms_scale(xf) * g_ref[...]).astype(BF16)
    for c in range(3):
        for blk in range(D_MODEL // D_NA):
            lo_col = c * D_MODEL + blk * D_NA
            y = jnp.dot(h, w_ref[:, lo_col:lo_col + D_NA], preferred_element_type=F32)
            if c == 0:
                y = y * (HEAD_DIM ** -0.5)
            if blk == 0:
                na_ref[:, c * D_NA:(c + 1) * D_NA] = y.astype(BF16)
            else:
                for half in range(D_NA // LANES):
                    slab[(blk - 1) * (D_NA // LANES) + half] = y[:, half * LANES:(half + 1) * LANES]
        for ref, dil in zip(cls_refs, DILATIONS):
            for j in range(N_DIL_SLABS):
                for r in range(dil):
                    rows = slab[j] if dil == 1 else slab.at[j][pl.ds(r, tm // dil, stride=dil), :]
                    ref[0, c * N_DIL_SLABS + j, r] = rows.astype(BF16)


def _qkv_proj(x2, b, s, gain, w_bf16, tm=512):
    n = x2.shape[0]
    tpb = s // tm
    cls_shapes = [jax.ShapeDtypeStruct((b, 3 * N_DIL_SLABS, d, s // d, LANES), BF16) for d in DILATIONS]
    cls_specs = [pl.BlockSpec((1, 3 * N_DIL_SLABS, d, tm // d, LANES), lambda i: (i // tpb, 0, 0, i % tpb, 0))
                 for d in DILATIONS]
    return pl.pallas_call(
        functools.partial(_qkv_kernel, tm=tm),
        out_shape=[jax.ShapeDtypeStruct((n, 3 * D_NA), BF16)] + cls_shapes,
        grid=(n // tm,),
        in_specs=[
            pl.BlockSpec((tm, D_MODEL), lambda i: (i, 0)),
            pl.BlockSpec((1, D_MODEL), lambda i: (0, 0)),
            pl.BlockSpec((D_MODEL, 3 * D_MODEL), lambda i: (0, 0)),
        ],
        out_specs=[pl.BlockSpec((tm, 3 * D_NA), lambda i: (i, 0))] + cls_specs,
        scratch_shapes=[pltpu.VMEM((N_DIL_SLABS, tm, LANES), F32)],
        compiler_params=_params("parallel"),
        name="qkv_proj",
    )(x2, gain.reshape(1, D_MODEL), w_bf16)


def _na_bias_table(rpb):
    qc = np.arange(GRID_W)[:, None]
    kc = np.arange(GRID_W)[None, :]
    win0 = np.clip(qc - NA_COLS // 2, 0, GRID_W - NA_COLS)
    in_win = (kc >= win0) & (kc < win0 + NA_COLS)
    ci = np.clip(kc - qc + NA_COLS - 1, 0, 2 * NA_COLS - 2)
    n_ri, n_ci = 2 * NA_ROWS - 1, 2 * NA_COLS - 1
    pick = (ci.reshape(1, -1) == np.arange(n_ci)[:, None]).astype(np.float32)
    m = jnp.dot(rpb.astype(F32).reshape(N_HEADS_NA * n_ri, n_ci), pick, precision=lax.Precision.HIGHEST)
    m = jnp.where(in_win[None, None], m.reshape(N_HEADS_NA, n_ri, GRID_W, GRID_W), NEG_BIG)
    t = jnp.stack([m[:, v:v + NA_ROWS] for v in range(NA_ROWS)], axis=1)
    t = jnp.transpose(t, (0, 1, 3, 2, 4))
    t = t.reshape(N_HEADS_NA // HEADS_PER_LANE_TILE, HEADS_PER_LANE_TILE, NA_ROWS, GRID_W, NA_ROWS * GRID_W)
    return jnp.transpose(t, (0, 2, 1, 3, 4)).reshape(
        N_HEADS_NA // HEADS_PER_LANE_TILE, NA_ROWS, HEADS_PER_LANE_TILE * GRID_W, NA_ROWS * GRID_W)


def _head_rms(o, gain_tile, lo):
    o2 = o * o
    ms_lo = jnp.sum(jnp.where(lo, o2, 0.0), axis=-1, keepdims=True) * (1.0 / HEAD_DIM)
    ms_hi = jnp.sum(jnp.where(lo, 0.0, o2), axis=-1, keepdims=True) * (1.0 / HEAD_DIM)
    scale = jnp.where(lo, lax.rsqrt(ms_lo + RMS_EPS), lax.rsqrt(ms_hi + RMS_EPS))
    return o * scale * gain_tile


def _na_kernel(q_ref, k_ref, v_ref, t_ref, g_ref, o_ref, *, rows):
    nk = NA_ROWS * GRID_W
    lo = lax.broadcasted_iota(jnp.int32, (1, LANES), 1) < HEAD_DIM

    n_slabs = D_NA // LANES
    per_trip = 2

    def row_body(rr, carry):
        where, scores, values = [], [], []
        for u in range(per_trip):
            r = rr * per_trip + u
            kr0 = jnp.clip(r - NA_ROWS // 2, 0, rows - NA_ROWS)
            q0 = pl.multiple_of(r * GRID_W, GRID_W)
            k0 = pl.multiple_of(kr0 * GRID_W, GRID_W)
            where.append((q0, kr0 - r + NA_ROWS - 1))
            for p in range(n_slabs):
                lanes = slice(p * LANES, (p + 1) * LANES)
                q2 = q_ref[0, pl.ds(q0, GRID_W), lanes]
                zero = jnp.zeros_like(q2)
                q_both = jnp.concatenate([jnp.where(lo, q2, zero), jnp.where(lo, zero, q2)], axis=0)
                scores.append(lax.dot_general(q_both, k_ref[0, pl.ds(k0, nk), lanes], (((1,), (1,)), ((), ())),
                                              preferred_element_type=F32))
                values.append(v_ref[0, pl.ds(k0, nk), lanes])
        probs, sums = [], []
        for u in range(per_trip):
            for p in range(n_slabs):
                s = scores[u * n_slabs + p] + t_ref[p, where[u][1]]
                e = jnp.exp(s - jnp.max(s, axis=-1, keepdims=True))
                sums.append(jnp.sum(e, axis=-1, keepdims=True))
                probs.append(e.astype(BF16))
        for u in range(per_trip):
            for p in range(n_slabs):
                lanes = slice(p * LANES, (p + 1) * LANES)
                o = jnp.dot(probs[u * n_slabs + p], values[u * n_slabs + p],
                            preferred_element_type=F32) / sums[u * n_slabs + p]
                o = jnp.where(lo, o[:GRID_W], o[GRID_W:])
                o_ref[0, pl.ds(where[u][0], GRID_W), lanes] = _head_rms(o, g_ref[:, lanes], lo).astype(BF16)
        return carry

    lax.fori_loop(0, rows // per_trip, row_body, 0)


def _na_attention(qkv, table, gain_na):
    b, s, _ = qkv.shape
    rows = s // GRID_W
    return pl.pallas_call(
        functools.partial(_na_kernel, rows=rows),
        out_shape=jax.ShapeDtypeStruct((b, s, D_NA), BF16),
        grid=(b,),
        in_specs=[
            pl.BlockSpec((1, s, D_NA), lambda i: (i, 0, 0)),
            pl.BlockSpec((1, s, D_NA), lambda i: (i, 0, 1)),
            pl.BlockSpec((1, s, D_NA), lambda i: (i, 0, 2)),
            pl.BlockSpec(table.shape, lambda i: (0, 0, 0, 0)),
            pl.BlockSpec((1, D_NA), lambda i: (0, 0)),
        ],
        out_specs=pl.BlockSpec((1, s, D_NA), lambda i: (i, 0, 0)),
        compiler_params=_params("parallel"),
        name="na_attention",
    )(qkv, qkv, qkv, table, gain_na.reshape(1, D_NA))


DIL_HALF = 64
SLABS_PER_GROUP = DIL_GROUP // LANES
HEADS_PER_GROUP = DIL_GROUP // HEAD_DIM


def _dil_kernel(q_ref, k_ref, v_ref, o_ref, lse_ref, bias, *, length, dilation, slopes, tq):
    kw = min(length, tq + 2 * DIL_HALF)
    n_off = 1 if length <= tq else 3
    lo = lax.broadcasted_iota(jnp.int32, (1, LANES), 1) < HEAD_DIM
    lane = lax.broadcasted_iota(jnp.int32, (1, LANES), 1)
    grp = pl.program_id(1)
    qi = lax.broadcasted_iota(jnp.int32, (tq, kw), 0)
    ki = lax.broadcasted_iota(jnp.int32, (tq, kw), 1)

    for v in range(n_off):
        dist = jnp.abs(qi + v * DIL_HALF - ki)
        reach = dist <= DIL_HALF
        distf = dist.astype(F32) * float(dilation)
        for hl in range(HEADS_PER_GROUP):
            slope = jnp.float32(slopes[hl])
            for g in range(1, N_DIL_GROUPS):
                slope = jnp.where(grp == g, jnp.float32(slopes[g * HEADS_PER_GROUP + hl]), slope)
            p, hh = divmod(hl, HEADS_PER_LANE_TILE)
            bias[v, p, hh * tq:(hh + 1) * tq, :] = jnp.where(reach, -slope * distf, NEG_BIG)

    @pl.when(grp == 0)
    def _():
        lse_ref[...] = jnp.zeros_like(lse_ref)

    n_blk = dilation * (length // tq)
    per_trip = 4 if n_blk % 4 == 0 else 1

    def blk_body(tt, carry):
        where, scores, values = [], [], []
        for u in range(per_trip):
            t = tt * per_trip + u
            r = t // (length // tq)
            q0 = pl.multiple_of((t % (length // tq)) * tq, tq)
            k0 = pl.multiple_of(jnp.clip(q0 - DIL_HALF, 0, length - kw), DIL_HALF)
            where.append((r, q0, (q0 - k0) // DIL_HALF))
            for p in range(SLABS_PER_GROUP):
                q2 = q_ref[0, p, r, pl.ds(q0, tq), :]
                k2 = k_ref[0, p, r, pl.ds(k0, kw), :]
                zero = jnp.zeros_like(q2)
                q_both = jnp.concatenate([jnp.where(lo, q2, zero), jnp.where(lo, zero, q2)], axis=0)
                scores.append(lax.dot_general(q_both, k2, (((1,), (1,)), ((), ())), preferred_element_type=F32))
                values.append(v_ref[0, p, r, pl.ds(k0, kw), :])
        probs, stats = [], []
        for u in range(per_trip):
            for p in range(SLABS_PER_GROUP):
                s = scores[u * SLABS_PER_GROUP + p] + bias[where[u][2], p]
                m = jnp.max(s, axis=-1, keepdims=True)
                e = jnp.exp(s - m)
                stats.append((m, jnp.sum(e, axis=-1, keepdims=True)))
                probs.append(e.astype(BF16))
        for u in range(per_trip):
            r, q0, _ = where[u]
            lse_tile = lse_ref[0, 0, r, pl.ds(q0, tq), :]
            for p in range(SLABS_PER_GROUP):
                m, l = stats[u * SLABS_PER_GROUP + p]
                o = jnp.dot(probs[u * SLABS_PER_GROUP + p], values[u * SLABS_PER_GROUP + p],
                            preferred_element_type=F32) / l
                o_ref[0, p, r, pl.ds(q0, tq), :] = jnp.where(lo, o[:tq], o[tq:]).astype(o_ref.dtype)
                lse = m + jnp.log(l)
                first = grp * HEADS_PER_GROUP + p * HEADS_PER_LANE_TILE
                lse_tile = jnp.where(lane == first, lse[:tq], jnp.where(lane == first + 1, lse[tq:], lse_tile))
            lse_ref[0, 0, r, pl.ds(q0, tq), :] = lse_tile
        return carry

    lax.fori_loop(0, n_blk // per_trip, blk_body, 0)


def _dil_attention(cls, dilation, slopes, tq=128):
    b, _, _, length, _ = cls.shape

    def spec(part):
        return pl.BlockSpec((1, SLABS_PER_GROUP, dilation, length, LANES),
                            lambda i, g: (i, part * N_DIL_GROUPS + g, 0, 0, 0))

    n_off = 1 if length <= tq else 3
    return pl.pallas_call(
        functools.partial(_dil_kernel, length=length, dilation=dilation,
                          slopes=tuple(float(x) for x in slopes), tq=tq),
        out_shape=(jax.ShapeDtypeStruct((b, N_DIL_SLABS, dilation, length, LANES), BF16),
                   jax.ShapeDtypeStruct((b, 1, dilation, length, LANES), F32)),
        grid=(b, N_DIL_GROUPS),
        in_specs=[spec(0), spec(1), spec(2)],
        out_specs=(pl.BlockSpec((1, SLABS_PER_GROUP, dilation, length, LANES), lambda i, g: (i, g, 0, 0, 0)),
                   pl.BlockSpec((1, 1, dilation, length, LANES), lambda i, g: (i, 0, 0, 0, 0))),
        scratch_shapes=[pltpu.VMEM((n_off, SLABS_PER_GROUP, HEADS_PER_LANE_TILE * tq,
                                    min(length, tq + 2 * DIL_HALF)), F32)],
        compiler_params=_params("parallel", "arbitrary"),
        name=f"dil_attention_d{dilation}",
    )(cls, cls, cls)


def _out_kernel(a_ref, *rest, tm):
    n_pat = len(DILATIONS)
    o_refs, l_refs = rest[:n_pat], rest[n_pat:2 * n_pat]
    g_ref, w_ref, e_ref, x_ref, y_ref, nat_o, nat_l, mixed = rest[2 * n_pat:]
    lo = lax.broadcasted_iota(jnp.int32, (1, LANES), 1) < HEAD_DIM

    for k, dil in enumerate(DILATIONS):
        if dil == 1:
            continue
        for r in range(dil):
            for j in range(N_DIL_SLABS):
                nat_o.at[k, j][pl.ds(r, tm // dil, stride=dil), :] = o_refs[k][0, j, r].astype(F32)
            nat_l.at[k][pl.ds(r, tm // dil, stride=dil), :] = l_refs[k][0, 0, r]

    def o_rows(k, j):
        return o_refs[k][0, j, 0].astype(F32) if DILATIONS[k] == 1 else nat_o[k, j]

    ls = [l_refs[k][0, 0, 0] if DILATIONS[k] == 1 else nat_l[k] for k in range(n_pat)]
    m = functools.reduce(jnp.maximum, ls)
    es = [jnp.exp(l - m) for l in ls]
    inv = 1.0 / functools.reduce(lambda u, v: u + v, es)
    spread = []
    for e in es:
        w = e * inv
        hi = w.astype(BF16)
        lo_part = (w - hi.astype(F32)).astype(BF16)
        spread.append(jnp.dot(hi, e_ref[...], preferred_element_type=F32)
                      + jnp.dot(lo_part, e_ref[...], preferred_element_type=F32))

    mixed[:, 0:D_NA] = a_ref[...]
    for j in range(N_DIL_SLABS):
        d = jnp.zeros((tm, LANES), F32)
        for k in range(n_pat):
            d = d + spread[k][:, j * LANES:(j + 1) * LANES] * o_rows(k, j)
        d = _head_rms(d, g_ref[:, j * LANES:(j + 1) * LANES], lo)
        mixed[:, D_NA + j * LANES:D_NA + (j + 1) * LANES] = d.astype(BF16)
    y_ref[...] = x_ref[...] + jnp.dot(mixed[...], w_ref[...], preferred_element_type=F32)


def _out_proj(a, os_, lses, s, gain_dil, w_bf16, x2, tm=256):
    n = x2.shape[0]
    tpb = s // tm
    row = lambda i: (i, 0)
    fixed = lambda i: (0, 0)
    cls = lambda i: (i // tpb, 0, 0, i % tpb, 0)
    head_cols = jnp.asarray(np.arange(LANES)[:, None] == np.arange(D_DIL)[None, :] // HEAD_DIM, BF16)
    return pl.pallas_call(
        functools.partial(_out_kernel, tm=tm),
        out_shape=jax.ShapeDtypeStruct((n, D_MODEL), F32),
        grid=(n // tm,),
        in_specs=[pl.BlockSpec((tm, D_NA), row)]
        + [pl.BlockSpec((1, N_DIL_SLABS, d, tm // d, LANES), cls) for d in DILATIONS]
        + [pl.BlockSpec((1, 1, d, tm // d, LANES), cls) for d in DILATIONS]
        + [pl.BlockSpec((1, D_DIL), fixed), pl.BlockSpec((D_MODEL, D_MODEL), fixed),
           pl.BlockSpec((LANES, D_DIL), fixed), pl.BlockSpec((tm, D_MODEL), row)],
        out_specs=pl.BlockSpec((tm, D_MODEL), row),
        scratch_shapes=[pltpu.VMEM((len(DILATIONS), N_DIL_SLABS, tm, LANES), F32),
                        pltpu.VMEM((len(DILATIONS), tm, LANES), F32),
                        pltpu.VMEM((tm, D_MODEL), BF16)],
        compiler_params=_params("parallel"),
        name="out_proj",
    )(a, *os_, *lses, gain_dil.reshape(1, D_DIL), w_bf16, head_cols, x2)


MOE_TILE = 256
MOE_CHUNK = 64
HEXT = D_MODEL + LANES
GATE_PIECES = 3


def _route_kernel(x_ref, g_ref, wr_ref, hext_ref, aff_ref):
    xf = x_ref[...]
    h = xf * _rms_scale(xf) * g_ref[...]
    h_hi = h.astype(BF16)
    hext_ref[:, :D_MODEL] = h_hi
    h_lo = (h - h_hi.astype(F32)).astype(BF16)
    w = wr_ref[...]
    w_hi = w.astype(BF16)
    w_lo = (w - w_hi.astype(F32)).astype(BF16)
    logits = (jnp.dot(h_hi, w_hi, preferred_element_type=F32) + jnp.dot(h_hi, w_lo, preferred_element_type=F32)
              + jnp.dot(h_lo, w_hi, preferred_element_type=F32))
    lane = lax.broadcasted_iota(jnp.int32, (1, LANES), 1)
    first = lane < N_EXPERTS
    used = lane < GATE_PIECES * N_EXPERTS
    m = jnp.max(jnp.where(first, logits, NEG_BIG), axis=-1, keepdims=True)
    e = jnp.exp(jnp.where(used, logits - m, 0.0))
    aff = e / jnp.sum(jnp.where(first, e, 0.0), axis=-1, keepdims=True)
    aff_ref[...] = aff[:, :N_EXPERTS]
    hi = aff.astype(BF16)
    r1 = aff - hi.astype(F32)
    mid = r1.astype(BF16)
    lo = (r1 - mid.astype(F32)).astype(BF16)
    pieces = jnp.where(first, hi, jnp.where(lane < 2 * N_EXPERTS, mid, lo))
    hext_ref[:, D_MODEL:] = jnp.where(used, pieces, jnp.zeros_like(pieces))


def _route(x2, gain, wr_pad, tm=512):
    n = x2.shape[0]
    return pl.pallas_call(
        _route_kernel,
        out_shape=(jax.ShapeDtypeStruct((n, HEXT), BF16),
                   jax.ShapeDtypeStruct((n, N_EXPERTS), F32)),
        grid=(n // tm,),
        in_specs=[pl.BlockSpec((tm, D_MODEL), lambda i: (i, 0)),
                  pl.BlockSpec((1, D_MODEL), lambda i: (0, 0)),
                  pl.BlockSpec((D_MODEL, LANES), lambda i: (0, 0))],
        out_specs=(pl.BlockSpec((tm, HEXT), lambda i: (i, 0)),
                   pl.BlockSpec((tm, N_EXPERTS), lambda i: (i, 0))),
        compiler_params=_params("parallel"),
        name="route",
    )(x2, gain.reshape(1, D_MODEL), wr_pad)


def _thr_kernel(aff_ref, thr_ref, need_ref, *, cap):
    bits = lax.bitcast_convert_type(aff_ref[...], jnp.int32)

    def count(mask):
        per_lane = jnp.sum(mask.astype(jnp.int32), axis=0)
        return jnp.sum(per_lane, axis=-1, keepdims=True)

    def body(i, prefix):
        cand = prefix | jnp.left_shift(jnp.int32(1), 30 - i)
        return jnp.where(count(bits >= cand[None]) >= cap, cand, prefix)

    thr = lax.fori_loop(0, 31, body, jnp.zeros((N_EXPERTS, 1), jnp.int32))
    thr_ref[...] = jnp.broadcast_to(thr, (N_EXPERTS, LANES))
    need_ref[...] = jnp.broadcast_to(cap - count(bits > thr[None]), (N_EXPERTS, LANES))


def _threshold(aff3, cap):
    out = jax.ShapeDtypeStruct((N_EXPERTS, LANES), jnp.int32)
    return pl.pallas_call(
        functools.partial(_thr_kernel, cap=cap),
        out_shape=(out, out),
        compiler_params=_params(),
        name="route_threshold",
    )(aff3)


def _pos_kernel(aff_ref, thr_ref, need_ref, tri_ref, dest_ref, start_ref, cnt_ref, run_eq, run_sel, *, tc):
    @pl.when(pl.program_id(0) == 0)
    def _():
        run_eq[...] = jnp.zeros_like(run_eq)
        run_sel[...] = jnp.zeros_like(run_sel)

    bits = lax.bitcast_convert_type(aff_ref[...], jnp.int32)
    thr = thr_ref[:, :1]
    need = need_ref[:, :1].astype(F32)
    gt = bits > thr[None]
    eq = bits == thr[None]

    def prefix(mask):
        f = mask.astype(F32)
        incl = jnp.dot(f.astype(BF16).reshape(tc * N_EXPERTS, MOE_TILE), tri_ref[...],
                       preferred_element_type=F32)
        return f, incl.reshape(tc, N_EXPERTS, MOE_TILE)

    eqf, eq_incl = prefix(eq)
    base = run_eq[:, :1]
    sels = []
    for j in range(tc):
        before = base + eq_incl[j] - eqf[j]
        sels.append(gt[j] | (eq[j] & (before < need)))
        base = base + eq_incl[j, :, MOE_TILE - 1:]
    run_eq[...] = jnp.broadcast_to(base, run_eq.shape)
    sel = jnp.stack(sels)
    _, incl = prefix(sel)
    dest_ref[...] = jnp.where(sel, incl - 1.0, -1.0).astype(jnp.int32)
    base = run_sel[:, :1]
    for j in range(tc):
        cnt = incl[j, :, MOE_TILE - 1:]
        start_ref[j] = jnp.broadcast_to(base, (N_EXPERTS, LANES)).astype(jnp.int32)
        cnt_ref[j] = jnp.broadcast_to(cnt, (N_EXPERTS, LANES)).astype(jnp.int32)
        base = base + cnt
    run_sel[...] = jnp.broadcast_to(base, run_sel.shape)


def _positions(aff3, thr, need, tc=8):
    nt = aff3.shape[0]
    tri = jnp.asarray(np.triu(np.ones((MOE_TILE, MOE_TILE), np.float32)), BF16)
    blk = lambda i: (i, 0, 0)
    fixed = lambda i: (0, 0)
    small = jax.ShapeDtypeStruct((nt, N_EXPERTS, LANES), jnp.int32)
    return pl.pallas_call(
        functools.partial(_pos_kernel, tc=tc),
        out_shape=(jax.ShapeDtypeStruct((nt, N_EXPERTS, MOE_TILE), jnp.int32), small, small),
        grid=(nt // tc,),
        in_specs=[pl.BlockSpec((tc, N_EXPERTS, MOE_TILE), blk),
                  pl.BlockSpec((N_EXPERTS, LANES), fixed), pl.BlockSpec((N_EXPERTS, LANES), fixed),
                  pl.BlockSpec((MOE_TILE, MOE_TILE), fixed)],
        out_specs=(pl.BlockSpec((tc, N_EXPERTS, MOE_TILE), blk),
                   pl.BlockSpec((tc, N_EXPERTS, LANES), blk), pl.BlockSpec((tc, N_EXPERTS, LANES), blk)),
        scratch_shapes=[pltpu.VMEM((N_EXPERTS, LANES), F32), pltpu.VMEM((N_EXPERTS, LANES), F32)],
        compiler_params=_params("arbitrary"),
        name="route_positions",
    )(aff3, thr, need, tri)


ROW_ALIGN = 16
MOE_WINDOW = MOE_CHUNK + ROW_ALIGN
MOE_STAGE_ROWS = N_EXPERTS * MOE_WINDOW


def _rounds(cnt_sm, tile):
    most = cnt_sm[tile * N_EXPERTS]
    for e in range(1, N_EXPERTS):
        most = jnp.maximum(most, cnt_sm[tile * N_EXPERTS + e])
    return (most + MOE_CHUNK - 1) // MOE_CHUNK


def _round_window(start_sm, cnt_sm, tile, e, r):
    base = jnp.minimum(r * MOE_CHUNK, cnt_sm[tile * N_EXPERTS + e])
    return base, start_sm[tile * N_EXPERTS + e] + base


def _align_down(v):
    return v & -ROW_ALIGN


def _align_rest(v):
    return v & (ROW_ALIGN - 1)


def _gather_kernel(start_sm, cnt_sm, dest_ref, hext_ref, xe_hbm, stage, sems, pad_sem, cur, busy, prev_row):
    i = pl.program_id(0)
    kio = lax.broadcasted_iota(jnp.int32, (MOE_WINDOW, MOE_TILE), 0)
    rid = lax.broadcasted_iota(jnp.int32, (ROW_ALIGN, HEXT), 0)

    def copy(e, s, row):
        return pltpu.make_async_copy(stage.at[s, e],
                                     xe_hbm.at[e, pl.ds(pl.multiple_of(row, ROW_ALIGN), MOE_WINDOW)], sems.at[s, e])

    @pl.when(i == 0)
    def _():
        for e in range(N_EXPERTS):
            cur[e] = 0
            busy[e] = 0
            prev_row[e] = 0
        stage[...] = jnp.zeros_like(stage)
        pads = [pltpu.make_async_copy(stage.at[0, e], xe_hbm.at[e, pl.ds(xe_hbm.shape[1] - MOE_WINDOW, MOE_WINDOW)],
                                      pad_sem) for e in range(N_EXPERTS)]
        for c in pads:
            c.start()
        for c in pads:
            c.wait()

    def onehot_rows(e, r):
        base, first = _round_window(start_sm, cnt_sm, i, e, r)
        local = dest_ref[0, e:e + 1, :] - base
        ok = (local >= 0) & (local < MOE_CHUNK)
        return (jnp.where(ok, local + _align_rest(first), -1) == kio).astype(F32).astype(BF16), first

    def place(e, first, rows):
        old_slot = cur[e]
        new_slot = 1 - old_slot
        aligned = _align_down(first)
        off = pl.multiple_of(aligned - prev_row[e], ROW_ALIGN)
        old = stage[old_slot, e, pl.ds(off, ROW_ALIGN), :]
        stage[new_slot, e, 0:ROW_ALIGN, :] = jnp.where(rid < _align_rest(first), old, rows[0:ROW_ALIGN])
        stage[new_slot, e, ROW_ALIGN:MOE_WINDOW, :] = rows[ROW_ALIGN:MOE_WINDOW]

        @pl.when(busy[e] > 0)
        def _():
            copy(e, old_slot, 0).wait()

        copy(e, new_slot, aligned).start()
        prev_row[e] = aligned
        cur[e] = new_slot
        busy[e] = 1

    picks = [onehot_rows(e, 0) for e in range(N_EXPERTS)]
    res = jnp.dot(jnp.concatenate([p[0] for p in picks], axis=0), hext_ref[...],
                  preferred_element_type=F32).astype(BF16)
    for e in range(N_EXPERTS):
        place(e, picks[e][1], res[e * MOE_WINDOW:(e + 1) * MOE_WINDOW])

    def extra_round(r, carry):
        for e in range(N_EXPERTS):
            @pl.when(cnt_sm[i * N_EXPERTS + e] > r * MOE_CHUNK)
            def _():
                onehot, first = onehot_rows(e, r)
                place(e, first, jnp.dot(onehot, hext_ref[...], preferred_element_type=F32).astype(BF16))
        return carry

    lax.fori_loop(1, _rounds(cnt_sm, i), extra_round, 0)

    @pl.when(i == pl.num_programs(0) - 1)
    def _():
        for e in range(N_EXPERTS):
            copy(e, cur[e], 0).wait()


def _gather(starts, cnts, dest3, hext, cap):
    nt = dest3.shape[0]
    return pl.pallas_call(
        _gather_kernel,
        out_shape=jax.ShapeDtypeStruct((N_EXPERTS, cap + MOE_WINDOW, HEXT), BF16),
        grid_spec=pltpu.PrefetchScalarGridSpec(
            num_scalar_prefetch=2,
            grid=(nt,),
            in_specs=[pl.BlockSpec((1, N_EXPERTS, MOE_TILE), lambda i, *_: (i, 0, 0)),
                      pl.BlockSpec((MOE_TILE, HEXT), lambda i, *_: (i, 0))],
            out_specs=pl.BlockSpec(memory_space=pl.ANY),
            scratch_shapes=[pltpu.VMEM((2, N_EXPERTS, MOE_WINDOW, HEXT), BF16),
                            pltpu.SemaphoreType.DMA((2, N_EXPERTS)), pltpu.SemaphoreType.DMA(()),
                            pltpu.SMEM((N_EXPERTS,), jnp.int32), pltpu.SMEM((N_EXPERTS,), jnp.int32),
                            pltpu.SMEM((N_EXPERTS,), jnp.int32)],
        ),
        compiler_params=_params("arbitrary"),
        name="route_gather",
    )(starts, cnts, dest3, hext)


def _ffn_kernel(xe_ref, wg_ref, wu_ref, wd_ref, ye_ref):
    xe = xe_ref[:, :D_MODEL]
    pieces = xe_ref[:, D_MODEL:].astype(F32)
    lane = lax.broadcasted_iota(jnp.int32, (1, LANES), 1)
    mine = (lane % N_EXPERTS == pl.program_id(0)) & (lane < GATE_PIECES * N_EXPERTS)
    g = jnp.sum(jnp.where(mine, pieces, 0.0), axis=-1, keepdims=True)
    gate = jnp.dot(xe, wg_ref[...], preferred_element_type=F32)
    up = jnp.dot(xe, wu_ref[...], preferred_element_type=F32)
    hid = (gate * jax.nn.sigmoid(gate) * up).astype(BF16)
    ye_ref[...] = (jnp.dot(hid, wd_ref[...], preferred_element_type=F32) * g).astype(BF16)


def _expert_ffn(xe, cap, wg, wu, wd, tm=512):
    tm = min(tm, cap)
    d = D_MODEL
    tile = lambda i, j: (i, j, 0)
    wspec = pl.BlockSpec((None, d, d), lambda i, j: (i, 0, 0))
    return pl.pallas_call(
        _ffn_kernel,
        out_shape=jax.ShapeDtypeStruct((N_EXPERTS, cap, d), BF16),
        grid=(N_EXPERTS, cap // tm),
        in_specs=[pl.BlockSpec((None, tm, HEXT), tile), wspec, wspec, wspec],
        out_specs=pl.BlockSpec((None, tm, d), tile),
        compiler_params=_params("parallel", "arbitrary"),
        name="expert_ffn",
    )(xe, wg, wu, wd)


def _combine_kernel(start_sm, cnt_sm, dest_ref, x_ref, ye_hbm, out_ref, stage, sems, spill, spill_sems, *, cap):
    i = pl.program_id(0)
    slot = i % 2
    lane = lax.broadcasted_iota(jnp.int32, (1, LANES), 1)
    lane_e = lax.broadcasted_iota(jnp.int32, (1, N_EXPERTS), 1)

    def window(tile, e, r):
        base, first = _round_window(start_sm, cnt_sm, tile, e, r)
        return base, first, jnp.minimum(_align_down(first), cap - MOE_WINDOW)

    def copy(e, s, row):
        return pltpu.make_async_copy(ye_hbm.at[e, pl.ds(pl.multiple_of(row, ROW_ALIGN), MOE_WINDOW)],
                                     stage.at[s, pl.ds(e * MOE_WINDOW, MOE_WINDOW)], sems.at[s])

    def fetch(tile, s, r):
        for e in range(N_EXPERTS):
            copy(e, s, window(tile, e, r)[2]).start()

    def wait_all(s):
        for e in range(N_EXPERTS):
            copy(e, s, 0).wait()

    def expand(r):
        lo = jnp.zeros((1, N_EXPERTS), jnp.int32)
        shift = jnp.zeros((1, N_EXPERTS), jnp.int32)
        for e in range(N_EXPERTS):
            base, first, row = window(i, e, r)
            lo = jnp.where(lane_e == e, base, lo)
            shift = jnp.where(lane_e == e, first - row + e * MOE_WINDOW - base, shift)
        rank = dest_ref[0]
        col = jnp.where((rank >= lo) & (rank < lo + MOE_CHUNK), rank + shift, -1)
        tiles = []
        for j in range(MOE_STAGE_ROWS // LANES):
            hit = None
            for e in range(j * LANES // MOE_WINDOW, min(N_EXPERTS - 1, ((j + 1) * LANES - 1) // MOE_WINDOW) + 1):
                m = col[:, e:e + 1] == lane + j * LANES
                hit = m if hit is None else hit | m
            tiles.append(hit)
        onehot = jnp.concatenate(tiles, axis=1).astype(F32).astype(BF16)
        return jnp.dot(onehot, stage[slot], preferred_element_type=F32)

    @pl.when(i == 0)
    def _():
        fetch(0, 0, 0)

    @pl.when(i + 1 < pl.num_programs(0))
    def _():
        fetch(i + 1, 1 - slot, 0)

    wait_all(slot)
    out_ref[...] = x_ref[...] + expand(0)

    @pl.when(i == 0)
    def _():
        spill[...] = jnp.zeros_like(spill)

    def spill_copy(e, row):
        return pltpu.make_async_copy(ye_hbm.at[e, pl.ds(pl.multiple_of(row, ROW_ALIGN), MOE_WINDOW)],
                                     spill.at[e, pl.ds(0, MOE_WINDOW)], spill_sems.at[e])

    def extra_round(r, carry):
        for e in range(N_EXPERTS):
            @pl.when(cnt_sm[i * N_EXPERTS + e] > r * MOE_CHUNK)
            def _():
                spill_copy(e, window(i, e, r)[2]).start()
        for e in range(N_EXPERTS):
            @pl.when(cnt_sm[i * N_EXPERTS + e] > r * MOE_CHUNK)
            def _():
                base, first, row = window(i, e, r)
                rank = dest_ref[0, :, e:e + 1]
                col = jnp.where((rank >= base) & (rank < base + MOE_CHUNK), rank - base + first - row, -1)
                spill_copy(e, 0).wait()
                out_ref[...] += jnp.dot((col == lane).astype(F32).astype(BF16), spill[e],
                                        preferred_element_type=F32)
        return carry

    lax.fori_loop(1, _rounds(cnt_sm, i), extra_round, 0)


def _combine(starts, cnts, dest_t, x2, ye, cap):
    nt = dest_t.shape[0]
    return pl.pallas_call(
        functools.partial(_combine_kernel, cap=cap),
        out_shape=jax.ShapeDtypeStruct(x2.shape, F32),
        grid_spec=pltpu.PrefetchScalarGridSpec(
            num_scalar_prefetch=2,
            grid=(nt,),
            in_specs=[pl.BlockSpec((1, MOE_TILE, N_EXPERTS), lambda i, *_: (i, 0, 0)),
                      pl.BlockSpec((MOE_TILE, D_MODEL), lambda i, *_: (i, 0)),
                      pl.BlockSpec(memory_space=pl.ANY)],
            out_specs=pl.BlockSpec((MOE_TILE, D_MODEL), lambda i, *_: (i, 0)),
            scratch_shapes=[pltpu.VMEM((2, MOE_STAGE_ROWS, D_MODEL), BF16), pltpu.SemaphoreType.DMA((2,)),
                            pltpu.VMEM((N_EXPERTS, LANES, D_MODEL), BF16), pltpu.SemaphoreType.DMA((N_EXPERTS,))],
        ),
        compiler_params=_params("arbitrary"),
        name="route_combine",
    )(starts, cnts, dest_t, x2, ye)


def _final_kernel(x_ref, g_ref, o_ref):
    xf = x_ref[...]
    o_ref[...] = xf * _rms_scale(xf) * g_ref[...]


def _final_norm(x2, gain, tm=1024):
    n = x2.shape[0]
    return pl.pallas_call(
        _final_kernel,
        out_shape=jax.ShapeDtypeStruct((n, D_MODEL), F32),
        grid=(n // tm,),
        in_specs=[pl.BlockSpec((tm, D_MODEL), lambda i: (i, 0)),
                  pl.BlockSpec((1, D_MODEL), lambda i: (0, 0))],
        out_specs=pl.BlockSpec((tm, D_MODEL), lambda i: (i, 0)),
        compiler_params=_params("parallel"),
        name="final_norm",
    )(x2, gain.reshape(1, D_MODEL))


def _mixer(x2, b, s, norm1, w_in, table, gain_na, gain_dil, w_out, slopes):
    na_qkv, *cls = _qkv_proj(x2, b, s, norm1, w_in)
    a = _na_attention(na_qkv.reshape(b, s, 3 * D_NA), table, gain_na).reshape(b * s, D_NA)
    os_, lses = [], []
    for c, dilation in zip(cls, DILATIONS):
        o, lse = _dil_attention(c, dilation, slopes)
        os_.append(o)
        lses.append(lse)
    return _out_proj(a, os_, lses, s, gain_dil, w_out, x2)


def _moe(x2, norm2, wr_pad, wg, wu, wd):
    n = x2.shape[0]
    cap = CAPACITY_FACTOR * n // N_EXPERTS
    nt = n // MOE_TILE
    hext, aff = _route(x2, norm2, wr_pad)
    aff3 = aff.reshape(nt, MOE_TILE, N_EXPERTS).transpose(0, 2, 1)
    thr, need = _threshold(aff3, cap)
    dest3, start3, cnt3 = _positions(aff3, thr, need)
    starts = start3[:, :, 0].reshape(-1)
    cnts = cnt3[:, :, 0].reshape(-1)
    xe = _gather(starts, cnts, dest3, hext, cap)
    ye = _expert_ffn(xe, cap, wg, wu, wd)
    return _combine(starts, cnts, dest3.transpose(0, 2, 1), x2, ye, cap)


def _trunk(x, norm1, w_in, tables, gain_na, gain_dil, w_out, norm2, w_router, wg, wu, wd, final_norm):
    b, s, _ = x.shape
    slopes = _alibi_slopes(N_HEADS_DIL)
    x2 = x.reshape(b * s, D_MODEL)
    for l in range(DEPTH):
        x2 = _mixer(x2, b, s, norm1[l], w_in[l], tables[l], gain_na[l], gain_dil[l], w_out[l], slopes)
        x2 = _moe(x2, norm2[l], w_router[l], wg[l], wu[l], wd[l])
    return _final_norm(x2, final_norm).reshape(b, s, D_MODEL)


def kernel(x_prompt, x_sample, norm1, w_in, rpb, gain_na, gain_dil, w_out, norm2, w_router, w_gate, w_up, w_down, final_norm):
    w_in_b = w_in.astype(BF16)
    w_out_b = w_out.astype(BF16)
    wg, wu, wd = w_gate.astype(BF16), w_up.astype(BF16), w_down.astype(BF16)
    tables = jax.vmap(_na_bias_table)(rpb)
    pad = jnp.zeros((DEPTH, D_MODEL, LANES - GATE_PIECES * N_EXPERTS), F32)
    wr_pad = jnp.concatenate([w_router] * GATE_PIECES + [pad], axis=-1)
    args = (norm1, w_in_b, tables, gain_na, gain_dil, w_out_b, norm2, wr_pad, wg, wu, wd, final_norm)
    return (_trunk(x_prompt, *args), _trunk(x_sample, *args))
```

```python
import functools
import math

import jax
import jax.numpy as jnp
import numpy as np
from jax import lax
from jax.experimental import pallas as pl
from jax.experimental.pallas import tpu as pltpu

D_MODEL = 1024
DEPTH = 4
HEAD_DIM = 64
N_HEADS = D_MODEL // HEAD_DIM
N_HEADS_NA = N_HEADS // 4
N_HEADS_DIL = N_HEADS - N_HEADS_NA
D_NA = N_HEADS_NA * HEAD_DIM
D_DIL = N_HEADS_DIL * HEAD_DIM
GRID_W = 64
NA_ROWS = 8
NA_COLS = 16
DIL_PATTERNS = ((128, 1), (512, 4), (2048, 16))
N_EXPERTS = 16
CAPACITY_FACTOR = 2
RMS_EPS = 1e-6

LANES = 128
HEADS_PER_LANE_TILE = LANES // HEAD_DIM
DIL_GROUP = 256
N_DIL_GROUPS = D_DIL // DIL_GROUP
NEG_BIG = -1e30
VMEM_LIMIT = 56 * 1024 * 1024

BF16 = jnp.bfloat16
F32 = jnp.float32


def _alibi_slopes(n):
    def pow2(m):
        start = 2.0 ** (-8.0 / m)
        return [start ** (i + 1) for i in range(m)]
    if math.log2(n).is_integer():
        s = pow2(n)
    else:
        c = 2 ** int(math.floor(math.log2(n)))
        s = pow2(c) + pow2(2 * c)[0::2][: n - c]
    return np.asarray(s, dtype=np.float32)


def _params(*sem):
    return pltpu.CompilerParams(dimension_semantics=sem, vmem_limit_bytes=VMEM_LIMIT)


def _rms_scale(xf):
    return lax.rsqrt(jnp.mean(xf * xf, axis=-1, keepdims=True) + RMS_EPS)


N_DIL_SLABS = D_DIL // LANES
DILATIONS = tuple(d for _, d in DIL_PATTERNS)


def _qkv_kernel(x_ref, g_ref, w_ref, na_ref, *rest, tm):
    cls_refs, slab = rest[:-1], rest[-1]
    xf = x_ref[...]
    h = (xf * _rms_scale(xf) * g_ref[...]).astype(BF16)
    for c in range(3):
        for blk in range(D_MODEL // D_NA):
            lo_col = c * D_MODEL + blk * D_NA
            y = jnp.dot(h, w_ref[:, lo_col:lo_col + D_NA], preferred_element_type=F32)
            if c == 0:
                y = y * (HEAD_DIM ** -0.5)
            if blk == 0:
                na_ref[:, c * D_NA:(c + 1) * D_NA] = y.astype(BF16)
            else:
                for half in range(D_NA // LANES):
                    slab[(blk - 1) * (D_NA // LANES) + half] = y[:, half * LANES:(half + 1) * LANES]
        for ref, dil in zip(cls_refs, DILATIONS):
            for j in range(N_DIL_SLABS):
                for r in range(dil):
                    rows = slab[j] if dil == 1 else slab.at[j][pl.ds(r, tm // dil, stride=dil), :]
                    ref[0, c * N_DIL_SLABS + j, r] = rows.astype(BF16)


def _qkv_proj(x2, b, s, gain, w_bf16, tm=512):
    n = x2.shape[0]
    tpb = s // tm
    cls_shapes = [jax.ShapeDtypeStruct((b, 3 * N_DIL_SLABS, d, s // d, LANES), BF16) for d in DILATIONS]
    cls_specs = [pl.BlockSpec((1, 3 * N_DIL_SLABS, d, tm // d, LANES), lambda i: (i // tpb, 0, 0, i % tpb, 0))
                 for d in DILATIONS]
    return pl.pallas_call(
        functools.partial(_qkv_kernel, tm=tm),
        out_shape=[jax.ShapeDtypeStruct((n, 3 * D_NA), BF16)] + cls_shapes,
        grid=(n // tm,),
        in_specs=[
            pl.BlockSpec((tm, D_MODEL), lambda i: (i, 0)),
            pl.BlockSpec((1, D_MODEL), lambda i: (0, 0)),
            pl.BlockSpec((D_MODEL, 3 * D_MODEL), lambda i: (0, 0)),
        ],
        out_specs=[pl.BlockSpec((tm, 3 * D_NA), lambda i: (i, 0))] + cls_specs,
        scratch_shapes=[pltpu.VMEM((N_DIL_SLABS, tm, LANES), F32)],
        compiler_params=_params("parallel"),
        name="qkv_proj",
    )(x2, gain.reshape(1, D_MODEL), w_bf16)


def _na_bias_table(rpb):
    qc = np.arange(GRID_W)[:, None]
    kc = np.arange(GRID_W)[None, :]
    win0 = np.clip(qc - NA_COLS // 2, 0, GRID_W - NA_COLS)
    in_win = (kc >= win0) & (kc < win0 + NA_COLS)
    ci = np.clip(kc - qc + NA_COLS - 1, 0, 2 * NA_COLS - 2)
    n_ri, n_ci = 2 * NA_ROWS - 1, 2 * NA_COLS - 1
    pick = (ci.reshape(1, -1) == np.arange(n_ci)[:, None]).astype(np.float32)
    m = jnp.dot(rpb.astype(F32).reshape(N_HEADS_NA * n_ri, n_ci), pick, precision=lax.Precision.HIGHEST)
    m = jnp.where(in_win[None, None], m.reshape(N_HEADS_NA, n_ri, GRID_W, GRID_W), NEG_BIG)
    t = jnp.stack([m[:, v:v + NA_ROWS] for v in range(NA_ROWS)], axis=1)
    t = jnp.transpose(t, (0, 1, 3, 2, 4))
    t = t.reshape(N_HEADS_NA // HEADS_PER_LANE_TILE, HEADS_PER_LANE_TILE, NA_ROWS, GRID_W, NA_ROWS * GRID_W)
    return jnp.transpose(t, (0, 2, 1, 3, 4)).reshape(
        N_HEADS_NA // HEADS_PER_LANE_TILE, NA_ROWS, HEADS_PER_LANE_TILE * GRID_W, NA_ROWS * GRID_W)


def _head_rms(o, gain_tile, lo):
    o2 = o * o
    ms_lo = jnp.sum(jnp.where(lo, o2, 0.0), axis=-1, keepdims=True) * (1.0 / HEAD_DIM)
    ms_hi = jnp.sum(jnp.where(lo, 0.0, o2), axis=-1, keepdims=True) * (1.0 / HEAD_DIM)
    scale = jnp.where(lo, lax.rsqrt(ms_lo + RMS_EPS), lax.rsqrt(ms_hi + RMS_EPS))
    return o * scale * gain_tile


def _na_kernel(q_ref, k_ref, v_ref, t_ref, g_ref, o_ref, *, rows):
    nk = NA_ROWS * GRID_W
    lo = lax.broadcasted_iota(jnp.int32, (1, LANES), 1) < HEAD_DIM

    n_slabs = D_NA // LANES
    per_trip = 2

    def row_body(rr, carry):
        where, scores, values = [], [], []
        for u in range(per_trip):
            r = rr * per_trip + u
            kr0 = jnp.clip(r - NA_ROWS // 2, 0, rows - NA_ROWS)
            q0 = pl.multiple_of(r * GRID_W, GRID_W)
            k0 = pl.multiple_of(kr0 * GRID_W, GRID_W)
            where.append((q0, kr0 - r + NA_ROWS - 1))
            for p in range(n_slabs):
                lanes = slice(p * LANES, (p + 1) * LANES)
                q2 = q_ref[0, pl.ds(q0, GRID_W), lanes]
                zero = jnp.zeros_like(q2)
                q_both = jnp.concatenate([jnp.where(lo, q2, zero), jnp.where(lo, zero, q2)], axis=0)
                scores.append(lax.dot_general(q_both, k_ref[0, pl.ds(k0, nk), lanes], (((1,), (1,)), ((), ())),
                                              preferred_element_type=F32))
                values.append(v_ref[0, pl.ds(k0, nk), lanes])
        probs, sums = [], []
        for u in range(per_trip):
            for p in range(n_slabs):
                s = scores[u * n_slabs + p] + t_ref[p, where[u][1]]
                e = jnp.exp(s - jnp.max(s, axis=-1, keepdims=True))
                sums.append(jnp.sum(e, axis=-1, keepdims=True))
                probs.append(e.astype(BF16))
        for u in range(per_trip):
            for p in range(n_slabs):
                lanes = slice(p * LANES, (p + 1) * LANES)
                o = jnp.dot(probs[u * n_slabs + p], values[u * n_slabs + p],
                            preferred_element_type=F32) / sums[u * n_slabs + p]
                o = jnp.where(lo, o[:GRID_W], o[GRID_W:])
                o_ref[0, pl.ds(where[u][0], GRID_W), lanes] = _head_rms(o, g_ref[:, lanes], lo).astype(BF16)
        return carry

    lax.fori_loop(0, rows // per_trip, row_body, 0)


def _na_attention(qkv, table, gain_na):
    b, s, _ = qkv.shape
    rows = s // GRID_W
    return pl.pallas_call(
        functools.partial(_na_kernel, rows=rows),
        out_shape=jax.ShapeDtypeStruct((b, s, D_NA), BF16),
        grid=(b,),
        in_specs=[
            pl.BlockSpec((1, s, D_NA), lambda i: (i, 0, 0)),
            pl.BlockSpec((1, s, D_NA), lambda i: (i, 0, 1)),
            pl.BlockSpec((1, s, D_NA), lambda i: (i, 0, 2)),
            pl.BlockSpec(table.shape, lambda i: (0, 0, 0, 0)),
            pl.BlockSpec((1, D_NA), lambda i: (0, 0)),
        ],
        out_specs=pl.BlockSpec((1, s, D_NA), lambda i: (i, 0, 0)),
        compiler_params=_params("parallel"),
        name="na_attention",
    )(qkv, qkv, qkv, table, gain_na.reshape(1, D_NA))


DIL_HALF = 64
SLABS_PER_GROUP = DIL_GROUP // LANES
HEADS_PER_GROUP = DIL_GROUP // HEAD_DIM


def _dil_kernel(q_ref, k_ref, v_ref, o_ref, lse_ref, bias, *, length, dilation, slopes, tq):
    kw = min(length, tq + 2 * DIL_HALF)
    n_off = 1 if length <= tq else 3
    lo = lax.broadcasted_iota(jnp.int32, (1, LANES), 1) < HEAD_DIM
    lane = lax.broadcasted_iota(jnp.int32, (1, LANES), 1)
    grp = pl.program_id(1)
    qi = lax.broadcasted_iota(jnp.int32, (tq, kw), 0)
    ki = lax.broadcasted_iota(jnp.int32, (tq, kw), 1)

    for v in range(n_off):
        dist = jnp.abs(qi + v * DIL_HALF - ki)
        reach = dist <= DIL_HALF
        distf = dist.astype(F32) * float(dilation)
        for hl in range(HEADS_PER_GROUP):
            slope = jnp.float32(slopes[hl])
            for g in range(1, N_DIL_GROUPS):
                slope = jnp.where(grp == g, jnp.float32(slopes[g * HEADS_PER_GROUP + hl]), slope)
            p, hh = divmod(hl, HEADS_PER_LANE_TILE)
            bias[v, p, hh * tq:(hh + 1) * tq, :] = jnp.where(reach, -slope * distf, NEG_BIG)

    @pl.when(grp == 0)
    def _():
        lse_ref[...] = jnp.zeros_like(lse_ref)

    n_blk = dilation * (length // tq)
    per_trip = 4 if n_blk % 4 == 0 else 1

    def blk_body(tt, carry):
        where, scores, values = [], [], []
        for u in range(per_trip):
            t = tt * per_trip + u
            r = t // (length // tq)
            q0 = pl.multiple_of((t % (length // tq)) * tq, tq)
            k0 = pl.multiple_of(jnp.clip(q0 - DIL_HALF, 0, length - kw), DIL_HALF)
            where.append((r, q0, (q0 - k0) // DIL_HALF))
            for p in range(SLABS_PER_GROUP):
                q2 = q_ref[0, p, r, pl.ds(q0, tq), :]
                k2 = k_ref[0, p, r, pl.ds(k0, kw), :]
                zero = jnp.zeros_like(q2)
                q_both = jnp.concatenate([jnp.where(lo, q2, zero), jnp.where(lo, zero, q2)], axis=0)
                scores.append(lax.dot_general(q_both, k2, (((1,), (1,)), ((), ())), preferred_element_type=F32))
                values.append(v_ref[0, p, r, pl.ds(k0, kw), :])
        probs, stats = [], []
        for u in range(per_trip):
            for p in range(SLABS_PER_GROUP):
                s = scores[u * SLABS_PER_GROUP + p] + bias[where[u][2], p]
                m = jnp.max(s, axis=-1, keepdims=True)
                e = jnp.exp(s - m)
                stats.append((m, jnp.sum(e, axis=-1, keepdims=True)))
                probs.append(e.astype(BF16))
        for u in range(per_trip):
            r, q0, _ = where[u]
            lse_tile = lse_ref[0, 0, r, pl.ds(q0, tq), :]
            for p in range(SLABS_PER_GROUP):
                m, l = stats[u * SLABS_PER_GROUP + p]
                o = jnp.dot(probs[u * SLABS_PER_GROUP + p], values[u * SLABS_PER_GROUP + p],
                            preferred_element_type=F32) / l
                o_ref[0, p, r, pl.ds(q0, tq), :] = jnp.where(lo, o[:tq], o[tq:]).astype(o_ref.dtype)
                lse = m + jnp.log(l)
                first = grp * HEADS_PER_GROUP + p * HEADS_PER_LANE_TILE
                lse_tile = jnp.where(lane == first, lse[:tq], jnp.where(lane == first + 1, lse[tq:], lse_tile))
            lse_ref[0, 0, r, pl.ds(q0, tq), :] = lse_tile
        return carry

    lax.fori_loop(0, n_blk // per_trip, blk_body, 0)


def _dil_attention(cls, dilation, slopes, tq=128):
    b, _, _, length, _ = cls.shape

    def spec(part):
        return pl.BlockSpec((1, SLABS_PER_GROUP, dilation, length, LANES),
                            lambda i, g: (i, part * N_DIL_GROUPS + g, 0, 0, 0))

    n_off = 1 if length <= tq else 3
    return pl.pallas_call(
        functools.partial(_dil_kernel, length=length, dilation=dilation,
                          slopes=tuple(float(x) for x in slopes), tq=tq),
        out_shape=(jax.ShapeDtypeStruct((b, N_DIL_SLABS, dilation, length, LANES), BF16),
                   jax.ShapeDtypeStruct((b, 1, dilation, length, LANES), F32)),
        grid=(b, N_DIL_GROUPS),
        in_specs=[spec(0), spec(1), spec(2)],
        out_specs=(pl.BlockSpec((1, SLABS_PER_GROUP, dilation, length, LANES), lambda i, g: (i, g, 0, 0, 0)),
                   pl.BlockSpec((1, 1, dilation, length, LANES), lambda i, g: (i, 0, 0, 0, 0))),
        scratch_shapes=[pltpu.VMEM((n_off, SLABS_PER_GROUP, HEADS_PER_LANE_TILE * tq,
                                    min(length, tq + 2 * DIL_HALF)), F32)],
        compiler_params=_params("parallel", "arbitrary"),
        name=f"dil_attention_d{dilation}",
    )(cls, cls, cls)


def _out_kernel(a_ref, *rest, tm):
    n_pat = len(DILATIONS)
    o_refs, l_refs = rest[:n_pat], rest[n_pat:2 * n_pat]
    g_ref, w_ref, e_ref, x_ref, y_ref, nat_o, nat_l, mixed = rest[2 * n_pat:]
    lo = lax.broadcasted_iota(jnp.int32, (1, LANES), 1) < HEAD_DIM

    for k, dil in enumerate(DILATIONS):
        if dil == 1:
            continue
        for r in range(dil):
            for j in range(N_DIL_SLABS):
                nat_o.at[k, j][pl.ds(r, tm // dil, stride=dil), :] = o_refs[k][0, j, r].astype(F32)
            nat_l.at[k][pl.ds(r, tm // dil, stride=dil), :] = l_refs[k][0, 0, r]

    def o_rows(k, j):
        return o_refs[k][0, j, 0].astype(F32) if DILATIONS[k] == 1 else nat_o[k, j]

    ls = [l_refs[k][0, 0, 0] if DILATIONS[k] == 1 else nat_l[k] for k in range(n_pat)]
    m = functools.reduce(jnp.maximum, ls)
    es = [jnp.exp(l - m) for l in ls]
    inv = 1.0 / functools.reduce(lambda u, v: u + v, es)
    spread = []
    for e in es:
        w = e * inv
        hi = w.astype(BF16)
        lo_part = (w - hi.astype(F32)).astype(BF16)
        spread.append(jnp.dot(hi, e_ref[...], preferred_element_type=F32)
                      + jnp.dot(lo_part, e_ref[...], preferred_element_type=F32))

    mixed[:, 0:D_NA] = a_ref[...]
    for j in range(N_DIL_SLABS):
        d = jnp.zeros((tm, LANES), F32)
        for k in range(n_pat):
            d = d + spread[k][:, j * LANES:(j + 1) * LANES] * o_rows(k, j)
        d = _head_rms(d, g_ref[:, j * LANES:(j + 1) * LANES], lo)
        mixed[:, D_NA + j * LANES:D_NA + (j + 1) * LANES] = d.astype(BF16)
    y_ref[...] = x_ref[...] + jnp.dot(mixed[...], w_ref[...], preferred_element_type=F32)


def _out_proj(a, os_, lses, s, gain_dil, w_bf16, x2, tm=256):
    n = x2.shape[0]
    tpb = s // tm
    row = lambda i: (i, 0)
    fixed = lambda i: (0, 0)
    cls = lambda i: (i // tpb, 0, 0, i % tpb, 0)
    head_cols = jnp.asarray(np.arange(LANES)[:, None] == np.arange(D_DIL)[None, :] // HEAD_DIM, BF16)
    return pl.pallas_call(
        functools.partial(_out_kernel, tm=tm),
        out_shape=jax.ShapeDtypeStruct((n, D_MODEL), F32),
        grid=(n // tm,),
        in_specs=[pl.BlockSpec((tm, D_NA), row)]
        + [pl.BlockSpec((1, N_DIL_SLABS, d, tm // d, LANES), cls) for d in DILATIONS]
        + [pl.BlockSpec((1, 1, d, tm // d, LANES), cls) for d in DILATIONS]
        + [pl.BlockSpec((1, D_DIL), fixed), pl.BlockSpec((D_MODEL, D_MODEL), fixed),
           pl.BlockSpec((LANES, D_DIL), fixed), pl.BlockSpec((tm, D_MODEL), row)],
        out_specs=pl.BlockSpec((tm, D_MODEL), row),
        scratch_shapes=[pltpu.VMEM((len(DILATIONS), N_DIL_SLABS, tm, LANES), F32),
                        pltpu.VMEM((len(DILATIONS), tm, LANES), F32),
                        pltpu.VMEM((tm, D_MODEL), BF16)],
        compiler_params=_params("parallel"),
        name="out_proj",
    )(a, *os_, *lses, gain_dil.reshape(1, D_DIL), w_bf16, head_cols, x2)


MOE_TILE = 256
MOE_CHUNK = 64
HEXT = D_MODEL + LANES
GATE_PIECES = 3


def _route_kernel(x_ref, g_ref, wr_ref, hext_ref, aff_ref):
    xf = x_ref[...]
    h = xf * _rms_scale(xf) * g_ref[...]
    h_hi = h.astype(BF16)
    hext_ref[:, :D_MODEL] = h_hi
    h_lo = (h - h_hi.astype(F32)).astype(BF16)
    w = wr_ref[...]
    w_hi = w.astype(BF16)
    w_lo = (w - w_hi.astype(F32)).astype(BF16)
    logits = (jnp.dot(h_hi, w_hi, preferred_element_type=F32) + jnp.dot(h_hi, w_lo, preferred_element_type=F32)
              + jnp.dot(h_lo, w_hi, preferred_element_type=F32))
    lane = lax.broadcasted_iota(jnp.int32, (1, LANES), 1)
    first = lane < N_EXPERTS
    used = lane < GATE_PIECES * N_EXPERTS
    m = jnp.max(jnp.where(first, logits, NEG_BIG), axis=-1, keepdims=True)
    e = jnp.exp(jnp.where(used, logits - m, 0.0))
    aff = e / jnp.sum(jnp.where(first, e, 0.0), axis=-1, keepdims=True)
    aff_ref[...] = aff[:, :N_EXPERTS]
    hi = aff.astype(BF16)
    r1 = aff - hi.astype(F32)
    mid = r1.astype(BF16)
    lo = (r1 - mid.astype(F32)).astype(BF16)
    pieces = jnp.where(first, hi, jnp.where(lane < 2 * N_EXPERTS, mid, lo))
    hext_ref[:, D_MODEL:] = jnp.where(used, pieces, jnp.zeros_like(pieces))


def _route(x2, gain, wr_pad, tm=512):
    n = x2.shape[0]
    return pl.pallas_call(
        _route_kernel,
        out_shape=(jax.ShapeDtypeStruct((n, HEXT), BF16),
                   jax.ShapeDtypeStruct((n, N_EXPERTS), F32)),
        grid=(n // tm,),
        in_specs=[pl.BlockSpec((tm, D_MODEL), lambda i: (i, 0)),
                  pl.BlockSpec((1, D_MODEL), lambda i: (0, 0)),
                  pl.BlockSpec((D_MODEL, LANES), lambda i: (0, 0))],
        out_specs=(pl.BlockSpec((tm, HEXT), lambda i: (i, 0)),
                   pl.BlockSpec((tm, N_EXPERTS), lambda i: (i, 0))),
        compiler_params=_params("parallel"),
        name="route",
    )(x2, gain.reshape(1, D_MODEL), wr_pad)


def _thr_kernel(aff_ref, thr_ref, need_ref, *, cap):
    bits = lax.bitcast_convert_type(aff_ref[...], jnp.int32)

    def count(mask):
        per_lane = jnp.sum(mask.astype(jnp.int32), axis=0)
        return jnp.sum(per_lane, axis=-1, keepdims=True)

    def body(i, prefix):
        cand = prefix | jnp.left_shift(jnp.int32(1), 30 - i)
        return jnp.where(count(bits >= cand[None]) >= cap, cand, prefix)

    thr = lax.fori_loop(0, 31, body, jnp.zeros((N_EXPERTS, 1), jnp.int32))
    thr_ref[...] = jnp.broadcast_to(thr, (N_EXPERTS, LANES))
    need_ref[...] = jnp.broadcast_to(cap - count(bits > thr[None]), (N_EXPERTS, LANES))


def _threshold(aff3, cap):
    out = jax.ShapeDtypeStruct((N_EXPERTS, LANES), jnp.int32)
    return pl.pallas_call(
        functools.partial(_thr_kernel, cap=cap),
        out_shape=(out, out),
        compiler_params=_params(),
        name="route_threshold",
    )(aff3)


def _pos_kernel(aff_ref, thr_ref, need_ref, tri_ref, dest_ref, start_ref, cnt_ref, run_eq, run_sel, *, tc):
    @pl.when(pl.program_id(0) == 0)
    def _():
        run_eq[...] = jnp.zeros_like(run_eq)
        run_sel[...] = jnp.zeros_like(run_sel)

    bits = lax.bitcast_convert_type(aff_ref[...], jnp.int32)
    thr = thr_ref[:, :1]
    need = need_ref[:, :1].astype(F32)
    gt = bits > thr[None]
    eq = bits == thr[None]

    def prefix(mask):
        f = mask.astype(F32)
        incl = jnp.dot(f.astype(BF16).reshape(tc * N_EXPERTS, MOE_TILE), tri_ref[...],
                       preferred_element_type=F32)
        return f, incl.reshape(tc, N_EXPERTS, MOE_TILE)

    eqf, eq_incl = prefix(eq)
    base = run_eq[:, :1]
    sels = []
    for j in range(tc):
        before = base + eq_incl[j] - eqf[j]
        sels.append(gt[j] | (eq[j] & (before < need)))
        base = base + eq_incl[j, :, MOE_TILE - 1:]
    run_eq[...] = jnp.broadcast_to(base, run_eq.shape)
    sel = jnp.stack(sels)
    _, incl = prefix(sel)
    dest_ref[...] = jnp.where(sel, incl - 1.0, -1.0).astype(jnp.int32)
    base = run_sel[:, :1]
    for j in range(tc):
        cnt = incl[j, :, MOE_TILE - 1:]
        start_ref[j] = jnp.broadcast_to(base, (N_EXPERTS, LANES)).astype(jnp.int32)
        cnt_ref[j] = jnp.broadcast_to(cnt, (N_EXPERTS, LANES)).astype(jnp.int32)
        base = base + cnt
    run_sel[...] = jnp.broadcast_to(base, run_sel.shape)


def _positions(aff3, thr, need, tc=8):
    nt = aff3.shape[0]
    tri = jnp.asarray(np.triu(np.ones((MOE_TILE, MOE_TILE), np.float32)), BF16)
    blk = lambda i: (i, 0, 0)
    fixed = lambda i: (0, 0)
    small = jax.ShapeDtypeStruct((nt, N_EXPERTS, LANES), jnp.int32)
    return pl.pallas_call(
        functools.partial(_pos_kernel, tc=tc),
        out_shape=(jax.ShapeDtypeStruct((nt, N_EXPERTS, MOE_TILE), jnp.int32), small, small),
        grid=(nt // tc,),
        in_specs=[pl.BlockSpec((tc, N_EXPERTS, MOE_TILE), blk),
                  pl.BlockSpec((N_EXPERTS, LANES), fixed), pl.BlockSpec((N_EXPERTS, LANES), fixed),
                  pl.BlockSpec((MOE_TILE, MOE_TILE), fixed)],
        out_specs=(pl.BlockSpec((tc, N_EXPERTS, MOE_TILE), blk),
                   pl.BlockSpec((tc, N_EXPERTS, LANES), blk), pl.BlockSpec((tc, N_EXPERTS, LANES), blk)),
        scratch_shapes=[pltpu.VMEM((N_EXPERTS, LANES), F32), pltpu.VMEM((N_EXPERTS, LANES), F32)],
        compiler_params=_params("arbitrary"),
        name="route_positions",
    )(aff3, thr, need, tri)


ROW_ALIGN = 16
MOE_WINDOW = MOE_CHUNK + ROW_ALIGN
MOE_STAGE_ROWS = N_EXPERTS * MOE_WINDOW


def _rounds(cnt_sm, tile):
    most = cnt_sm[tile * N_EXPERTS]
    for e in range(1, N_EXPERTS):
        most = jnp.maximum(most, cnt_sm[tile * N_EXPERTS + e])
    return (most + MOE_CHUNK - 1) // MOE_CHUNK


def _round_window(start_sm, cnt_sm, tile, e, r):
    base = jnp.minimum(r * MOE_CHUNK, cnt_sm[tile * N_EXPERTS + e])
    return base, start_sm[tile * N_EXPERTS + e] + base


def _align_down(v):
    return v & -ROW_ALIGN


def _align_rest(v):
    return v & (ROW_ALIGN - 1)


def _gather_kernel(start_sm, cnt_sm, dest_ref, hext_ref, xe_hbm, stage, sems, pad_sem, cur, busy, prev_row):
    i = pl.program_id(0)
    kio = lax.broadcasted_iota(jnp.int32, (MOE_WINDOW, MOE_TILE), 0)
    rid = lax.broadcasted_iota(jnp.int32, (ROW_ALIGN, HEXT), 0)

    def copy(e, s, row):
        return pltpu.make_async_copy(stage.at[s, e],
                                     xe_hbm.at[e, pl.ds(pl.multiple_of(row, ROW_ALIGN), MOE_WINDOW)], sems.at[s, e])

    @pl.when(i == 0)
    def _():
        for e in range(N_EXPERTS):
            cur[e] = 0
            busy[e] = 0
            prev_row[e] = 0
        stage[...] = jnp.zeros_like(stage)
        pads = [pltpu.make_async_copy(stage.at[0, e], xe_hbm.at[e, pl.ds(xe_hbm.shape[1] - MOE_WINDOW, MOE_WINDOW)],
                                      pad_sem) for e in range(N_EXPERTS)]
        for c in pads:
            c.start()
        for c in pads:
            c.wait()

    def onehot_rows(e, r):
        base, first = _round_window(start_sm, cnt_sm, i, e, r)
        local = dest_ref[0, e:e + 1, :] - base
        ok = (local >= 0) & (local < MOE_CHUNK)
        return (jnp.where(ok, local + _align_rest(first), -1) == kio).astype(F32).astype(BF16), first

    def place(e, first, rows):
        old_slot = cur[e]
        new_slot = 1 - old_slot
        aligned = _align_down(first)
        off = pl.multiple_of(aligned - prev_row[e], ROW_ALIGN)
        old = stage[old_slot, e, pl.ds(off, ROW_ALIGN), :]
        stage[new_slot, e, 0:ROW_ALIGN, :] = jnp.where(rid < _align_rest(first), old, rows[0:ROW_ALIGN])
        stage[new_slot, e, ROW_ALIGN:MOE_WINDOW, :] = rows[ROW_ALIGN:MOE_WINDOW]

        @pl.when(busy[e] > 0)
        def _():
            copy(e, old_slot, 0).wait()

        copy(e, new_slot, aligned).start()
        prev_row[e] = aligned
        cur[e] = new_slot
        busy[e] = 1

    picks = [onehot_rows(e, 0) for e in range(N_EXPERTS)]
    res = jnp.dot(jnp.concatenate([p[0] for p in picks], axis=0), hext_ref[...],
                  preferred_element_type=F32).astype(BF16)
    for e in range(N_EXPERTS):
        place(e, picks[e][1], res[e * MOE_WINDOW:(e + 1) * MOE_WINDOW])

    def extra_round(r, carry):
        for e in range(N_EXPERTS):
            @pl.when(cnt_sm[i * N_EXPERTS + e] > r * MOE_CHUNK)
            def _():
                onehot, first = onehot_rows(e, r)
                place(e, first, jnp.dot(onehot, hext_ref[...], preferred_element_type=F32).astype(BF16))
        return carry

    lax.fori_loop(1, _rounds(cnt_sm, i), extra_round, 0)

    @pl.when(i == pl.num_programs(0) - 1)
    def _():
        for e in range(N_EXPERTS):
            copy(e, cur[e], 0).wait()


def _gather(starts, cnts, dest3, hext, cap):
    nt = dest3.shape[0]
    return pl.pallas_call(
        _gather_kernel,
        out_shape=jax.ShapeDtypeStruct((N_EXPERTS, cap + MOE_WINDOW, HEXT), BF16),
        grid_spec=pltpu.PrefetchScalarGridSpec(
            num_scalar_prefetch=2,
            grid=(nt,),
            in_specs=[pl.BlockSpec((1, N_EXPERTS, MOE_TILE), lambda i, *_: (i, 0, 0)),
                      pl.BlockSpec((MOE_TILE, HEXT), lambda i, *_: (i, 0))],
            out_specs=pl.BlockSpec(memory_space=pl.ANY),
            scratch_shapes=[pltpu.VMEM((2, N_EXPERTS, MOE_WINDOW, HEXT), BF16),
                            pltpu.SemaphoreType.DMA((2, N_EXPERTS)), pltpu.SemaphoreType.DMA(()),
                            pltpu.SMEM((N_EXPERTS,), jnp.int32), pltpu.SMEM((N_EXPERTS,), jnp.int32),
                            pltpu.SMEM((N_EXPERTS,), jnp.int32)],
        ),
        compiler_params=_params("arbitrary"),
        name="route_gather",
    )(starts, cnts, dest3, hext)


def _ffn_kernel(xe_ref, wg_ref, wu_ref, wd_ref, ye_ref):
    xe = xe_ref[:, :D_MODEL]
    pieces = xe_ref[:, D_MODEL:].astype(F32)
    lane = lax.broadcasted_iota(jnp.int32, (1, LANES), 1)
    mine = (lane % N_EXPERTS == pl.program_id(0)) & (lane < GATE_PIECES * N_EXPERTS)
    g = jnp.sum(jnp.where(mine, pieces, 0.0), axis=-1, keepdims=True)
    gate = jnp.dot(xe, wg_ref[...], preferred_element_type=F32)
    up = jnp.dot(xe, wu_ref[...], preferred_element_type=F32)
    hid = (gate * jax.nn.sigmoid(gate) * up).astype(BF16)
    ye_ref[...] = (jnp.dot(hid, wd_ref[...], preferred_element_type=F32) * g).astype(BF16)


def _expert_ffn(xe, cap, wg, wu, wd, tm=512):
    tm = min(tm, cap)
    d = D_MODEL
    tile = lambda i, j: (i, j, 0)
    wspec = pl.BlockSpec((None, d, d), lambda i, j: (i, 0, 0))
    return pl.pallas_call(
        _ffn_kernel,
        out_shape=jax.ShapeDtypeStruct((N_EXPERTS, cap, d), BF16),
        grid=(N_EXPERTS, cap // tm),
        in_specs=[pl.BlockSpec((None, tm, HEXT), tile), wspec, wspec, wspec],
        out_specs=pl.BlockSpec((None, tm, d), tile),
        compiler_params=_params("parallel", "arbitrary"),
        name="expert_ffn",
    )(xe, wg, wu, wd)


def _combine_kernel(start_sm, cnt_sm, dest_ref, x_ref, ye_hbm, out_ref, stage, sems, spill, spill_sems, *, cap):
    i = pl.program_id(0)
    slot = i % 2
    lane = lax.broadcasted_iota(jnp.int32, (1, LANES), 1)
    lane_e = lax.broadcasted_iota(jnp.int32, (1, N_EXPERTS), 1)

    def window(tile, e, r):
        base, first = _round_window(start_sm, cnt_sm, tile, e, r)
        return base, first, jnp.minimum(_align_down(first), cap - MOE_WINDOW)

    def copy(e, s, row):
        return pltpu.make_async_copy(ye_hbm.at[e, pl.ds(pl.multiple_of(row, ROW_ALIGN), MOE_WINDOW)],
                                     stage.at[s, pl.ds(e * MOE_WINDOW, MOE_WINDOW)], sems.at[s])

    def fetch(tile, s, r):
        for e in range(N_EXPERTS):
            copy(e, s, window(tile, e, r)[2]).start()

    def wait_all(s):
        for e in range(N_EXPERTS):
            copy(e, s, 0).wait()

    def expand(r):
        lo = jnp.zeros((1, N_EXPERTS), jnp.int32)
        shift = jnp.zeros((1, N_EXPERTS), jnp.int32)
        for e in range(N_EXPERTS):
            base, first, row = window(i, e, r)
            lo = jnp.where(lane_e == e, base, lo)
            shift = jnp.where(lane_e == e, first - row - base, shift)
        rank = dest_ref[0]
        col = jnp.where((rank >= lo) & (rank < lo + MOE_CHUNK), rank + shift, -1)
        owner = lax.broadcasted_iota(jnp.int32, (N_EXPERTS, MOE_STAGE_ROWS), 0) * MOE_WINDOW
        at = lax.broadcasted_iota(jnp.int32, (N_EXPERTS, MOE_STAGE_ROWS), 1)
        mine = (at >= owner) & (at < owner + MOE_WINDOW)
        within = jnp.sum(jnp.where(mine, at - owner, 0), axis=0, keepdims=True).astype(F32)
        want = jnp.dot(col.astype(F32).astype(BF16), mine.astype(F32).astype(BF16), preferred_element_type=F32)
        onehot = (want == within).astype(F32).astype(BF16)
        return jnp.dot(onehot, stage[slot], preferred_element_type=F32)

    @pl.when(i == 0)
    def _():
        fetch(0, 0, 0)

    @pl.when(i + 1 < pl.num_programs(0))
    def _():
        fetch(i + 1, 1 - slot, 0)

    wait_all(slot)
    out_ref[...] = x_ref[...] + expand(0)

    @pl.when(i == 0)
    def _():
        spill[...] = jnp.zeros_like(spill)

    def spill_copy(e, row):
        return pltpu.make_async_copy(ye_hbm.at[e, pl.ds(pl.multiple_of(row, ROW_ALIGN), MOE_WINDOW)],
                                     spill.at[e, pl.ds(0, MOE_WINDOW)], spill_sems.at[e])

    def extra_round(r, carry):
        for e in range(N_EXPERTS):
            @pl.when(cnt_sm[i * N_EXPERTS + e] > r * MOE_CHUNK)
            def _():
                spill_copy(e, window(i, e, r)[2]).start()
        for e in range(N_EXPERTS):
            @pl.when(cnt_sm[i * N_EXPERTS + e] > r * MOE_CHUNK)
            def _():
                base, first, row = window(i, e, r)
                rank = dest_ref[0, :, e:e + 1]
                col = jnp.where((rank >= base) & (rank < base + MOE_CHUNK), rank - base + first - row, -1)
                spill_copy(e, 0).wait()
                out_ref[...] += jnp.dot((col == lane).astype(F32).astype(BF16), spill[e],
                                        preferred_element_type=F32)
        return carry

    lax.fori_loop(1, _rounds(cnt_sm, i), extra_round, 0)


def _combine(starts, cnts, dest_t, x2, ye, cap):
    nt = dest_t.shape[0]
    return pl.pallas_call(
        functools.partial(_combine_kernel, cap=cap),
        out_shape=jax.ShapeDtypeStruct(x2.shape, F32),
        grid_spec=pltpu.PrefetchScalarGridSpec(
            num_scalar_prefetch=2,
            grid=(nt,),
            in_specs=[pl.BlockSpec((1, MOE_TILE, N_EXPERTS), lambda i, *_: (i, 0, 0)),
                      pl.BlockSpec((MOE_TILE, D_MODEL), lambda i, *_: (i, 0)),
                      pl.BlockSpec(memory_space=pl.ANY)],
            out_specs=pl.BlockSpec((MOE_TILE, D_MODEL), lambda i, *_: (i, 0)),
            scratch_shapes=[pltpu.VMEM((2, MOE_STAGE_ROWS, D_MODEL), BF16), pltpu.SemaphoreType.DMA((2,)),
                            pltpu.VMEM((N_EXPERTS, LANES, D_MODEL), BF16), pltpu.SemaphoreType.DMA((N_EXPERTS,))],
        ),
        compiler_params=_params("arbitrary"),
        name="route_combine",
    )(starts, cnts, dest_t, x2, ye)


def _final_kernel(x_ref, g_ref, o_ref):
    xf = x_ref[...]
    o_ref[...] = xf * _rms_scale(xf) * g_ref[...]


def _final_norm(x2, gain, tm=1024):
    n = x2.shape[0]
    return pl.pallas_call(
        _final_kernel,
        out_shape=jax.ShapeDtypeStruct((n, D_MODEL), F32),
        grid=(n // tm,),
        in_specs=[pl.BlockSpec((tm, D_MODEL), lambda i: (i, 0)),
                  pl.BlockSpec((1, D_MODEL), lambda i: (0, 0))],
        out_specs=pl.BlockSpec((tm, D_MODEL), lambda i: (i, 0)),
        compiler_params=_params("parallel"),
        name="final_norm",
    )(x2, gain.reshape(1, D_MODEL))


def _mixer(x2, b, s, norm1, w_in, table, gain_na, gain_dil, w_out, slopes):
    na_qkv, *cls = _qkv_proj(x2, b, s, norm1, w_in)
    a = _na_attention(na_qkv.reshape(b, s, 3 * D_NA), table, gain_na).reshape(b * s, D_NA)
    os_, lses = [], []
    for c, dilation in zip(cls, DILATIONS):
        o, lse = _dil_attention(c, dilation, slopes)
        os_.append(o)
        lses.append(lse)
    return _out_proj(a, os_, lses, s, gain_dil, w_out, x2)


def _moe(x2, norm2, wr_pad, wg, wu, wd):
    n = x2.shape[0]
    cap = CAPACITY_FACTOR * n // N_EXPERTS
    nt = n // MOE_TILE
    hext, aff = _route(x2, norm2, wr_pad)
    aff3 = aff.reshape(nt, MOE_TILE, N_EXPERTS).transpose(0, 2, 1)
    thr, need = _threshold(aff3, cap)
    dest3, start3, cnt3 = _positions(aff3, thr, need)
    starts = start3[:, :, 0].reshape(-1)
    cnts = cnt3[:, :, 0].reshape(-1)
    xe = _gather(starts, cnts, dest3, hext, cap)
    ye = _expert_ffn(xe, cap, wg, wu, wd)
    return _combine(starts, cnts, dest3.transpose(0, 2, 1), x2, ye, cap)


def _trunk(x, norm1, w_in, tables, gain_na, gain_dil, w_out, norm2, w_router, wg, wu, wd, final_norm):
    b, s, _ = x.shape
    slopes = _alibi_slopes(N_HEADS_DIL)
    x2 = x.reshape(b * s, D_MODEL)
    for l in range(DEPTH):
        x2 = _mixer(x2, b, s, norm1[l], w_in[l], tables[l], gain_na[l], gain_dil[l], w_out[l], slopes)
        x2 = _moe(x2, norm2[l], w_router[l], wg[l], wu[l], wd[l])
    return _final_norm(x2, final_norm).reshape(b, s, D_MODEL)


def kernel(x_prompt, x_sample, norm1, w_in, rpb, gain_na, gain_dil, w_out, norm2, w_router, w_gate, w_up, w_down, final_norm):
    w_in_b = w_in.astype(BF16)
    w_out_b = w_out.astype(BF16)
    wg, wu, wd = w_gate.astype(BF16), w_up.astype(BF16), w_down.astype(BF16)
    tables = jax.vmap(_na_bias_table)(rpb)
    pad = jnp.zeros((DEPTH, D_MODEL, LANES - GATE_PIECES * N_EXPERTS), F32)
    wr_pad = jnp.concatenate([w_router] * GATE_PIECES + [pad], axis=-1)
    args = (norm1, w_in_b, tables, gain_na, gain_dil, w_out_b, norm2, wr_pad, wg, wu, wd, final_norm)
    return (_trunk(x_prompt, *args), _trunk(x_sample, *args))
```
